```python
import math
import jax, jax.numpy as jnp
from jax import lax
import numpy as np

D_MODEL = 2048
BATCH = 4
SEQ = 2048
DEPTH = 1
DEC_BATCH = 128
DEC_SEQ = 1
PAST_LEN = 16384
PAGE_SIZE = 128

A_CHUNK = 128
A_GROUP_DIM = 128
A_GROUPS = D_MODEL // A_GROUP_DIM
D_A = A_GROUPS * A_GROUP_DIM
B_HEADS = 16
B_HEAD_DIM = 128
D_B = B_HEADS * B_HEAD_DIM
CONV_W = 4
DELTA_CHUNK = 64
D_FF = 4 * D_MODEL
D_IN = 2 * D_A + 4 * D_B + 2 * B_HEADS + 2 * D_MODEL
ALPHA = (2.0 * DEPTH) ** 0.25
BETA_INIT = (8.0 * DEPTH) ** -0.25
LN_EPS = 1e-5
RMS_EPS = 1e-6

kernel_name = "hybrid_gmlp_gated_deltanet_deepnorm_step"


def layer_norm(x, g, b):
    xf = x.astype(jnp.float32)
    mu = jnp.mean(xf, axis=-1, keepdims=True)
    var = jnp.mean(jnp.square(xf - mu), axis=-1, keepdims=True)
    return ((xf - mu) * lax.rsqrt(var + LN_EPS) * g + b).astype(x.dtype)


def l2norm(x):
    xf = x.astype(jnp.float32)
    return (xf * lax.rsqrt(jnp.sum(jnp.square(xf), axis=-1, keepdims=True) + RMS_EPS)).astype(x.dtype)


def gated_rms_norm(o, z, w):
    of = o.astype(jnp.float32)
    of = of * lax.rsqrt(jnp.mean(jnp.square(of), axis=-1, keepdims=True) + RMS_EPS)
    return (of * w * jax.nn.silu(z.astype(jnp.float32))).astype(o.dtype)


def chunk_spatial_gate(u, v, w_s, b_s):
    b, t, _ = v.shape
    pad = (-t) % A_CHUNK
    nc = (t + pad) // A_CHUNK
    vp = jnp.pad(v, ((0, 0), (0, pad), (0, 0))).reshape(b, nc, A_CHUNK, A_GROUPS, A_GROUP_DIM)
    idx = jnp.arange(A_CHUNK)
    causal = idx[:, None] >= idx[None, :]
    w = jnp.where(causal[None], w_s, 0.0)
    mixed = jnp.einsum('gts,bcsgd->bctgd', w, vp) + jnp.transpose(b_s)[None, None, :, :, None]
    mixed = mixed.reshape(b, nc * A_CHUNK, D_A)[:, :t]
    return u * mixed.astype(u.dtype)


def gated_delta_rule(q, k, v, beta, g, s0):
    b, t, h, dk = q.shape
    dv = v.shape[-1]
    out_dtype = v.dtype
    pad = (-t) % DELTA_CHUNK
    nc = (t + pad) // DELTA_CHUNK

    def blocks(a):
        a = a.astype(jnp.float32)
        a = jnp.pad(a, [(0, 0), (0, pad)] + [(0, 0)] * (a.ndim - 2))
        a = a.reshape((b, nc, DELTA_CHUNK) + a.shape[2:])
        return jnp.moveaxis(a, 3, 1)

    q, k, v, beta, g = blocks(q), blocks(k), blocks(v), blocks(beta), blocks(g)
    gc = jnp.cumsum(g, axis=-1)
    idx = jnp.arange(DELTA_CHUNK)
    incl = idx[:, None] >= idx[None, :]
    strict = idx[:, None] > idx[None, :]
    decay = jnp.exp(jnp.where(incl, gc[..., :, None] - gc[..., None, :], -jnp.inf))
    kb = k * beta[..., None]
    vb = v * beta[..., None]
    n = jnp.where(strict, jnp.einsum('bhnid,bhnjd->bhnij', kb, k) * decay, 0.0)
    eye = jnp.eye(DELTA_CHUNK, dtype=jnp.float32)
    tmat = lax.linalg.triangular_solve(eye + n, jnp.broadcast_to(eye, n.shape),
                                       left_side=True, lower=True, unit_diagonal=True)
    w_val = jnp.einsum('bhnij,bhnje->bhnie', tmat, vb)
    k_cd = jnp.einsum('bhnij,bhnjd->bhnid', tmat, kb * jnp.exp(gc)[..., None])
    attn = jnp.einsum('bhnid,bhnjd->bhnij', q, k) * decay
    q_dec = q * jnp.exp(gc)[..., None]
    k_dec = k * jnp.exp(gc[..., -1:] - gc)[..., None]
    g_last = jnp.exp(gc[..., -1])
    xs = (jnp.moveaxis(w_val, 2, 0), jnp.moveaxis(k_cd, 2, 0), jnp.moveaxis(attn, 2, 0),
          jnp.moveaxis(q_dec, 2, 0), jnp.moveaxis(k_dec, 2, 0), jnp.moveaxis(g_last, 2, 0))

    def step(s, xc):
        w_c, kcd_c, attn_c, qd_c, kd_c, gl_c = xc
        v_new = w_c - jnp.einsum('bhid,bhde->bhie', kcd_c, s)
        o_c = jnp.einsum('bhid,bhde->bhie', qd_c, s) + jnp.einsum('bhij,bhje->bhie', attn_c, v_new)
        s = s * gl_c[..., None, None] + jnp.einsum('bhid,bhie->bhde', kd_c, v_new)
        return s, o_c

    s_fin, o = lax.scan(step, s0.astype(jnp.float32), xs)
    o = jnp.transpose(o, (1, 0, 3, 2, 4)).reshape(b, nc * DELTA_CHUNK, h, dv)[:, :t]
    return o.astype(out_dtype), s_fin.astype(s0.dtype)


def hybrid_layer(x, conv_buf, s0, w_in, w_s, b_s, ln_v_g, ln_v_b, w_conv, a_log, dt_bias, w_onorm,
                 w_proj_a, w_proj_b, w_o, ln1_g, ln1_b, w_up, w_down, ln2_g, ln2_b):
    b, t, _ = x.shape
    p = jnp.einsum('btd,de->bte', x, w_in)
    sizes = [D_A, D_A, 3 * D_B, D_B, B_HEADS, B_HEADS, D_MODEL, D_MODEL]
    points = [int(s) for s in np.cumsum(sizes)[:-1]]
    u, va, qkv, z, beta_logit, a_logit, gate_a, gate_b = jnp.split(p, points, axis=-1)
    u = jax.nn.gelu(u, approximate=False)
    va = layer_norm(jax.nn.gelu(va, approximate=False), ln_v_g, ln_v_b)
    y_a = chunk_spatial_gate(u, va, w_s, b_s)
    xp = jnp.concatenate([conv_buf.astype(qkv.dtype), qkv], axis=1)
    new_conv = xp[:, t:]
    acc = xp[:, 0:t] * w_conv[0]
    for j in range(1, CONV_W):
        acc = acc + xp[:, j:j + t] * w_conv[j]
    qkv_c = jax.nn.silu(acc)
    q, k, vb = jnp.split(qkv_c, 3, axis=-1)
    q = l2norm(q.reshape(b, t, B_HEADS, B_HEAD_DIM)) * (B_HEAD_DIM ** -0.5)
    k = l2norm(k.reshape(b, t, B_HEADS, B_HEAD_DIM))
    vb = vb.reshape(b, t, B_HEADS, B_HEAD_DIM)
    beta = jax.nn.sigmoid(beta_logit.astype(jnp.float32))
    g = -jnp.exp(a_log.astype(jnp.float32)) * jax.nn.softplus(a_logit.astype(jnp.float32) + dt_bias.astype(jnp.float32))
    o, s_new = gated_delta_rule(q, k, vb, beta, g, s0)
    y_b = gated_rms_norm(o, z.reshape(b, t, B_HEADS, B_HEAD_DIM), w_onorm).reshape(b, t, D_B)
    m = jax.nn.sigmoid(gate_a) * (y_a @ w_proj_a) + jax.nn.sigmoid(gate_b) * (y_b @ w_proj_b)
    x1 = layer_norm(ALPHA * x + m @ w_o, ln1_g, ln1_b)
    h = jnp.square(jax.nn.relu(x1 @ w_up)) @ w_down
    y = layer_norm(ALPHA * x1 + h, ln2_g, ln2_b)
    return y, va, new_conv, s_new


def setup_inputs(seed: int = 0) -> dict:
    key = jax.random.key(seed)
    ks = jax.random.split(key, 24)

    def nrm(k, shape, scale):
        return jax.random.normal(k, shape, jnp.float32) * scale

    x_prompt = nrm(ks[0], (BATCH, SEQ, D_MODEL), 1.0)
    x_sample = nrm(ks[1], (DEC_BATCH, DEC_SEQ, D_MODEL), 1.0)
    state_conv = nrm(ks[2], (DEPTH, DEC_BATCH, CONV_W - 1, 3 * D_B), 1.0)
    state_ssm = nrm(ks[3], (DEPTH, DEC_BATCH, B_HEADS, B_HEAD_DIM, B_HEAD_DIM), 0.1)
    w_in = nrm(ks[4], (DEPTH, D_MODEL, D_IN), D_MODEL ** -0.5)
    w_s = nrm(ks[5], (DEPTH, A_GROUPS, A_CHUNK, A_CHUNK), A_CHUNK ** -0.5)
    b_s = 1.0 + nrm(ks[6], (DEPTH, A_GROUPS, A_CHUNK), 0.01)
    ln_v_g = 1.0 + nrm(ks[7], (DEPTH, D_A), 0.02)
    ln_v_b = nrm(ks[8], (DEPTH, D_A), 0.02)
    w_conv = nrm(ks[9], (DEPTH, CONV_W, 3 * D_B), CONV_W ** -0.5)
    a_log = jnp.log(jax.random.uniform(ks[10], (DEPTH, B_HEADS), jnp.float32, minval=1.0, maxval=16.0))
    dt = jnp.exp(jax.random.uniform(ks[11], (DEPTH, B_HEADS), jnp.float32,
                                    minval=math.log(1e-3), maxval=math.log(1e-1)))
    dt_bias = dt + jnp.log(-jnp.expm1(-dt))
    w_onorm = 1.0 + nrm(ks[12], (DEPTH, B_HEAD_DIM), 0.02)
    w_proj_a = nrm(ks[13], (DEPTH, D_A, D_MODEL), D_A ** -0.5)
    w_proj_b = nrm(ks[14], (DEPTH, D_B, D_MODEL), D_B ** -0.5)
    w_o = nrm(ks[15], (DEPTH, D_MODEL, D_MODEL), BETA_INIT * D_MODEL ** -0.5)
    ln1_g = 1.0 + nrm(ks[16], (DEPTH, D_MODEL), 0.02)
    ln1_b = nrm(ks[17], (DEPTH, D_MODEL), 0.02)
    w_up = nrm(ks[18], (DEPTH, D_MODEL, D_FF), D_MODEL ** -0.5)
    w_down = nrm(ks[19], (DEPTH, D_FF, D_MODEL), BETA_INIT * D_FF ** -0.5)
    ln2_g = 1.0 + nrm(ks[20], (DEPTH, D_MODEL), 0.02)
    ln2_b = nrm(ks[21], (DEPTH, D_MODEL), 0.02)
    return {"x_prompt": x_prompt, "x_sample": x_sample, "state_conv": state_conv, "state_ssm": state_ssm,
            "w_in": w_in, "w_s": w_s, "b_s": b_s, "ln_v_g": ln_v_g, "ln_v_b": ln_v_b, "w_conv": w_conv,
            "a_log": a_log, "dt_bias": dt_bias, "w_onorm": w_onorm, "w_proj_a": w_proj_a,
            "w_proj_b": w_proj_b, "w_o": w_o, "ln1_g": ln1_g, "ln1_b": ln1_b, "w_up": w_up,
            "w_down": w_down, "ln2_g": ln2_g, "ln2_b": ln2_b}


def reference(x_prompt, x_sample, state_conv, state_ssm, w_in, w_s, b_s, ln_v_g, ln_v_b, w_conv, a_log,
              dt_bias, w_onorm, w_proj_a, w_proj_b, w_o, ln1_g, ln1_b, w_up, w_down, ln2_g, ln2_b):
    yp = x_prompt
    ys = x_sample
    bp = x_prompt.shape[0]
    conv_p, ssm_p, vrows_s, conv_s, ssm_s = [], [], [], [], []
    for l in range(DEPTH):
        lw = (w_in[l], w_s[l], b_s[l], ln_v_g[l], ln_v_b[l], w_conv[l], a_log[l], dt_bias[l], w_onorm[l],
              w_proj_a[l], w_proj_b[l], w_o[l], ln1_g[l], ln1_b[l], w_up[l], w_down[l], ln2_g[l], ln2_b[l])
        conv0 = jnp.zeros((bp, CONV_W - 1, 3 * D_B), x_prompt.dtype)
        ssm0 = jnp.zeros((bp, B_HEADS, B_HEAD_DIM, B_HEAD_DIM), x_prompt.dtype)
        yp, _, cp, sp = hybrid_layer(yp, conv0, ssm0, *lw)
        ys, vs, cs, ss = hybrid_layer(ys, state_conv[l], state_ssm[l], *lw)
        conv_p.append(cp)
        ssm_p.append(sp)
        vrows_s.append(vs)
        conv_s.append(cs)
        ssm_s.append(ss)
    new_conv_prompt = jnp.stack(conv_p)
    new_ssm_prompt = jnp.stack(ssm_p)
    new_gmlp_v_sample = jnp.stack(vrows_s)
    new_conv_sample = jnp.stack(conv_s)
    new_ssm_sample = jnp.stack(ssm_s)
    return (yp, ys, new_conv_prompt, new_ssm_prompt, new_gmlp_v_sample, new_conv_sample, new_ssm_sample)
```

```python
import functools

import jax
import jax.numpy as jnp
from jax import lax
from jax.experimental import pallas as pl
from jax.experimental.pallas import tpu as pltpu

F32 = jnp.float32
BF16 = jnp.bfloat16

D_MODEL = 2048
HEADS = 16
HEAD_DIM = 128
GROUPS = 16
A_CHUNK = 128
DELTA_CHUNK = 64
CONV_W = 4
D_FF = 4 * D_MODEL
ALPHA = 2.0 ** 0.25
LN_EPS = 1e-5
RMS_EPS = 1e-6

SEG = 2048
N_MAIN = 8 * SEG
(SEG_U, SEG_VA, SEG_Q, SEG_K, SEG_V, SEG_Z, SEG_GA, SEG_GB) = range(8)
BA_LANES = 128
VMEM_LIMIT = 48 * 1024 * 1024


def _params(*sem):
    return pltpu.CompilerParams(dimension_semantics=sem, vmem_limit_bytes=VMEM_LIMIT)


def _bdot(a, b):
    return jnp.dot(a.astype(BF16), b.astype(BF16), preferred_element_type=F32)


def _gelu(x):
    return 0.5 * x * (1.0 + lax.erf(x * (2.0 ** -0.5)))


def _sigmoid(x):
    return 1.0 / (1.0 + jnp.exp(-x))


def _silu(x):
    return x * _sigmoid(x)


def _softplus(x):
    return jnp.maximum(x, 0.0) + jnp.log1p(jnp.exp(-jnp.abs(x)))


def _layer_norm(x, g, b):
    mu = jnp.mean(x, axis=-1, keepdims=True)
    xc = x - mu
    var = jnp.mean(xc * xc, axis=-1, keepdims=True)
    return xc * lax.rsqrt(var + LN_EPS) * g + b


def _inproj_kernel(x_ref, w_ref, o_ref, *, gelu_panels):
    acc = jnp.dot(x_ref[...], w_ref[...], preferred_element_type=F32)
    j = pl.program_id(0)

    @pl.when(j < gelu_panels)
    def _():
        o_ref[...] = _gelu(acc).astype(o_ref.dtype)

    @pl.when(j >= gelu_panels)
    def _():
        o_ref[...] = acc.astype(o_ref.dtype)


def _inproj(x, w, out_dtype, gelu_cols, tm, tn):
    m, k = x.shape
    n = w.shape[1]
    return pl.pallas_call(
        functools.partial(_inproj_kernel, gelu_panels=gelu_cols // tn),
        out_shape=jax.ShapeDtypeStruct((m, n), out_dtype),
        grid=(n // tn, m // tm),
        in_specs=[pl.BlockSpec((tm, k), lambda j, i: (i, 0)),
                  pl.BlockSpec((k, tn), lambda j, i: (0, j))],
        out_specs=pl.BlockSpec((tm, tn), lambda j, i: (i, j)),
        compiler_params=_params("arbitrary", "arbitrary"),
        name="inproj",
    )(x, w)


def _mixer_a_kernel(u_ref, va_ref, ws_ref, bst_ref, g_ref, b_ref, ya_ref, *van_ref):
    van = _layer_norm(va_ref[...].astype(F32), g_ref[...], b_ref[...])
    if van_ref:
        van_ref[0][...] = van
    vb = van.astype(BF16)
    row = lax.broadcasted_iota(jnp.int32, (A_CHUNK, A_CHUNK), 0)
    col = lax.broadcasted_iota(jnp.int32, (A_CHUNK, A_CHUNK), 1)
    causal = row >= col
    for g in range(GROUPS):
        sl = slice(g * A_CHUNK, (g + 1) * A_CHUNK)
        w = jnp.where(causal, ws_ref[g], 0.0).astype(BF16)
        mixed = jnp.dot(w, vb[:, sl], preferred_element_type=F32) + bst_ref[:, g:g + 1]
        ya_ref[:, sl] = (u_ref[:, sl].astype(F32) * mixed).astype(ya_ref.dtype)


def _mixer_a(p, w_s, b_s_t, ln_g, ln_b, emit_van):
    m = p.shape[0]
    out_shape = [jax.ShapeDtypeStruct((m, SEG), BF16)]
    out_specs = [pl.BlockSpec((A_CHUNK, SEG), lambda i: (i, 0))]
    if emit_van:
        out_shape.append(jax.ShapeDtypeStruct((m, SEG), F32))
        out_specs.append(pl.BlockSpec((A_CHUNK, SEG), lambda i: (i, 0)))
    return pl.pallas_call(
        _mixer_a_kernel,
        out_shape=out_shape,
        grid=(m // A_CHUNK,),
        in_specs=[pl.BlockSpec((A_CHUNK, SEG), lambda i: (i, SEG_U)),
                  pl.BlockSpec((A_CHUNK, SEG), lambda i: (i, SEG_VA)),
                  pl.BlockSpec((GROUPS, A_CHUNK, A_CHUNK), lambda i: (0, 0, 0)),
                  pl.BlockSpec((A_CHUNK, GROUPS), lambda i: (0, 0)),
                  pl.BlockSpec((1, SEG), lambda i: (0, 0)),
                  pl.BlockSpec((1, SEG), lambda i: (0, 0))],
        out_specs=out_specs,
        compiler_params=_params("arbitrary"),
        name="mixer_a",
    )(p, p, w_s, b_s_t, ln_g, ln_b)


def _beta_and_g(ba, alog_row, dt_row):
    beta = _sigmoid(ba)
    g = -jnp.exp(alog_row) * _softplus(ba + dt_row)
    return beta, g


def _gated_rms_norm(o, z, w):
    o = o * lax.rsqrt(jnp.mean(o * o, axis=-1, keepdims=True) + RMS_EPS)
    return o * w * _silu(z)


def _l2norm(x):
    return x * lax.rsqrt(jnp.sum(x * x, axis=-1, keepdims=True) + RMS_EPS)


def _unit_lower_inverse(n, masks):
    eye, blk8, merges = masks
    nd = jnp.where(blk8, n, 0.0)
    n2 = _bdot(nd, nd)
    n4 = _bdot(n2, n2)
    p = eye - nd
    p = p + _bdot(p, n2)
    p = p + _bdot(p, n4)
    for cmask in merges:
        c = jnp.where(cmask, n, 0.0)
        p = p - _bdot(p, _bdot(c, p))
    return p


def _inverse_masks():
    c = DELTA_CHUNK
    r = lax.broadcasted_iota(jnp.int32, (c, c), 0)
    q = lax.broadcasted_iota(jnp.int32, (c, c), 1)
    eye = (r == q).astype(F32)
    blk = lambda s: (r >> s) == (q >> s)
    merges = [blk(s + 1) & jnp.logical_not(blk(s)) for s in (3, 4, 5)]
    return (eye, blk(3), merges), r >= q, r > q


def _delta_chunk_kernel(q_ref, k_ref, v_ref, z_ref, ba_ref, wq_ref, wk_ref, wv_ref, alog_ref, dt_ref,
                        won_ref, y_ref, s_out_ref, s_ref, ext_ref):
    c = pl.program_id(1)
    nc = pl.num_programs(1)
    C = DELTA_CHUNK

    @pl.when(c == 0)
    def _():
        s_ref[...] = jnp.zeros_like(s_ref)
        ext_ref[:, 0:8, :] = jnp.zeros((3, 8, SEG), F32)

    def conv_silu(x_ref, w_ref, seg):
        cur = x_ref[...].astype(F32)
        ext_ref[seg, 8:8 + C, :] = cur
        acc = cur * w_ref[3:4, :]
        for j in range(CONV_W - 1):
            acc = acc + ext_ref[seg, 5 + j:5 + j + C, :] * w_ref[j:j + 1, :]
        ext_ref[seg, 0:8, :] = cur[C - 8:C, :]
        return _silu(acc)

    qa = conv_silu(q_ref, wq_ref, 0)
    ka = conv_silu(k_ref, wk_ref, 1)
    va = conv_silu(v_ref, wv_ref, 2)

    beta, g = _beta_and_g(ba_ref[...], alog_ref[...], dt_ref[...])
    row = lax.broadcasted_iota(jnp.int32, (C, BA_LANES), 0)
    gc = g
    for s in (1, 2, 4, 8, 16, 32):
        gc = gc + jnp.where(row >= s, pltpu.roll(gc, s, axis=0), 0.0)
    gc_t = gc.T
    g_last = gc[C - 1:C, :]
    e_gc = jnp.exp(gc)
    e_rest = jnp.exp(g_last - gc)
    e_last = jnp.exp(g_last)

    inv_masks, incl, strict = _inverse_masks()
    w_on = won_ref[...]

    for h in range(HEADS):
        sl = slice(h * HEAD_DIM, (h + 1) * HEAD_DIM)
        a = HEADS + h
        q = _l2norm(qa[:, sl]) * (HEAD_DIM ** -0.5)
        k = _l2norm(ka[:, sl])
        v = va[:, sl]
        b_col = beta[:, h:h + 1]
        decay = jnp.exp(jnp.where(incl, gc[:, a:a + 1] - gc_t[a:a + 1, :], -jnp.inf))
        kb = k * b_col
        kk = lax.dot_general(jnp.concatenate([kb, q], axis=0).astype(BF16), k.astype(BF16),
                             (((1,), (1,)), ((), ())), preferred_element_type=F32)
        n = jnp.where(strict, kk[:C] * decay, 0.0)
        attn = kk[C:] * decay
        t = _unit_lower_inverse(n, inv_masks)
        e_col = e_gc[:, a:a + 1]
        tr = _bdot(t, jnp.concatenate([v * b_col, kb * e_col], axis=1))
        w_val, k_cd = tr[:, :HEAD_DIM], tr[:, HEAD_DIM:]
        s = s_ref[h]
        m1 = _bdot(jnp.concatenate([k_cd, q * e_col], axis=0), s)
        v_new = w_val - m1[:C]
        o = m1[C:] + _bdot(attn, v_new)
        k_dec = k * e_rest[:, a:a + 1]
        s_ref[h] = s * e_last[:, a:a + 1] + lax.dot_general(
            k_dec.astype(BF16), v_new.astype(BF16), (((0,), (0,)), ((), ())), preferred_element_type=F32)
        y_ref[:, sl] = _gated_rms_norm(o, z_ref[:, sl].astype(F32), w_on).astype(y_ref.dtype)

    @pl.when(c == nc - 1)
    def _():
        s_out_ref[0] = s_ref[...]


def _delta_chunks(p, ba, w_conv, alog_row, dt_row, w_onorm, batch, seq):
    C = DELTA_CHUNK
    nc = seq // C
    rows = lambda seg: pl.BlockSpec((C, SEG), lambda b, c: (b * nc + c, seg))
    wspec = lambda seg: pl.BlockSpec((CONV_W, SEG), lambda b, c: (0, seg))
    small = pl.BlockSpec((1, BA_LANES), lambda b, c: (0, 0))
    return pl.pallas_call(
        _delta_chunk_kernel,
        out_shape=[jax.ShapeDtypeStruct((batch * seq, SEG), BF16),
                   jax.ShapeDtypeStruct((batch, HEADS, HEAD_DIM, HEAD_DIM), F32)],
        grid=(batch, nc),
        in_specs=[rows(SEG_Q), rows(SEG_K), rows(SEG_V), rows(SEG_Z),
                  pl.BlockSpec((C, BA_LANES), lambda b, c: (b * nc + c, 0)),
                  wspec(0), wspec(1), wspec(2), small, small, small],
        out_specs=[pl.BlockSpec((C, SEG), lambda b, c: (b * nc + c, 0)),
                   pl.BlockSpec((1, HEADS, HEAD_DIM, HEAD_DIM), lambda b, c: (b, 0, 0, 0))],
        scratch_shapes=[pltpu.VMEM((HEADS, HEAD_DIM, HEAD_DIM), F32),
                        pltpu.VMEM((3, 8 + C, SEG), F32)],
        compiler_params=_params("arbitrary", "arbitrary"),
        name="delta_chunks",
    )(p, p, p, p, ba, w_conv, w_conv, w_conv, alog_row, dt_row, w_onorm)


ROWS = 8


def _delta_step_kernel(q_ref, k_ref, v_ref, z_ref, ba_ref, cs_ref, wq_ref, wk_ref, wv_ref, alog_ref,
                       dt_ref, won_ref, s_ref, y_ref, s_out_ref):
    r = pl.program_id(0) % ROWS
    rsel = lax.broadcasted_iota(jnp.int32, (ROWS, 1), 0) == r

    def pick(x):
        return jnp.sum(jnp.where(rsel, x.astype(F32), 0.0), axis=0, keepdims=True)

    def conv_silu(x_ref, w_ref, seg):
        acc = pick(x_ref[...]) * w_ref[3:4, :]
        for j in range(CONV_W - 1):
            acc = acc + cs_ref[0, j:j + 1, seg * SEG:(seg + 1) * SEG] * w_ref[j:j + 1, :]
        return _silu(acc)

    qa = conv_silu(q_ref, wq_ref, 0)
    ka = conv_silu(k_ref, wk_ref, 1)
    va = conv_silu(v_ref, wv_ref, 2)
    z = pick(z_ref[...])
    beta, g = _beta_and_g(pick(ba_ref[...]), alog_ref[...], dt_ref[...])
    e_g = jnp.exp(g)
    w_on = won_ref[...]
    pad = jnp.zeros((ROWS - 2, HEAD_DIM), F32)

    outs = []
    for h in range(HEADS):
        sl = slice(h * HEAD_DIM, (h + 1) * HEAD_DIM)
        a = HEADS + h
        q = _l2norm(qa[:, sl]) * (HEAD_DIM ** -0.5)
        k = _l2norm(ka[:, sl])
        v = va[:, sl]
        b_h = beta[:, h:h + 1]
        eg_h = e_g[:, a:a + 1]
        s = s_ref[0, h]
        ks_qs = _bdot(jnp.concatenate([k, q, pad], axis=0), s)
        v_new = b_h * (v - eg_h * ks_qs[0:1])
        qk = jnp.sum(q * k, axis=-1, keepdims=True)
        o = eg_h * ks_qs[1:2] + qk * v_new
        kv = lax.dot_general(jnp.concatenate([k, pad, jnp.zeros((1, HEAD_DIM), F32)], axis=0).astype(BF16),
                             jnp.broadcast_to(v_new, (ROWS, HEAD_DIM)).astype(BF16),
                             (((0,), (0,)), ((), ())), preferred_element_type=F32)
        s_out_ref[0, h] = s * eg_h + kv
        outs.append(_gated_rms_norm(o, z[:, sl], w_on))
    y_row = jnp.concatenate(outs, axis=1)

    @pl.when(r == 0)
    def _():
        y_ref[...] = jnp.broadcast_to(y_row, y_ref.shape).astype(y_ref.dtype)

    @pl.when(r != 0)
    def _():
        y_ref[...] = jnp.where(rsel, y_row, y_ref[...].astype(F32)).astype(y_ref.dtype)


def _delta_step(p, ba, conv_state, ssm_state, w_conv, alog_row, dt_row, w_onorm):
    n = p.shape[0]
    rows = lambda seg: pl.BlockSpec((ROWS, SEG), lambda b: (b // ROWS, seg))
    wspec = lambda seg: pl.BlockSpec((CONV_W, SEG), lambda b: (0, seg))
    small = pl.BlockSpec((1, BA_LANES), lambda b: (0, 0))
    sspec = pl.BlockSpec((1, HEADS, HEAD_DIM, HEAD_DIM), lambda b: (b, 0, 0, 0))
    return pl.pallas_call(
        _delta_step_kernel,
        out_shape=[jax.ShapeDtypeStruct((n, SEG), BF16),
                   jax.ShapeDtypeStruct(ssm_state.shape, F32)],
        grid=(n,),
        in_specs=[rows(SEG_Q), rows(SEG_K), rows(SEG_V), rows(SEG_Z),
                  pl.BlockSpec((ROWS, BA_LANES), lambda b: (b // ROWS, 0)),
                  pl.BlockSpec((1, CONV_W - 1, 3 * SEG), lambda b: (b, 0, 0)),
                  wspec(0), wspec(1), wspec(2), small, small, small, sspec],
        out_specs=[pl.BlockSpec((ROWS, SEG), lambda b: (b // ROWS, 0)), sspec],
        compiler_params=_params("arbitrary"),
        name="delta_step",
    )(p, p, p, p, ba, conv_state, w_conv, w_conv, w_conv, alog_row, dt_row, w_onorm, ssm_state)


def _merge_kernel(ya_ref, yb_ref, ga_ref, gb_ref, wa_ref, wb_ref, m_ref):
    pa = jnp.dot(ya_ref[...], wa_ref[...], preferred_element_type=F32)
    pb = jnp.dot(yb_ref[...], wb_ref[...], preferred_element_type=F32)
    m = _sigmoid(ga_ref[...].astype(F32)) * pa + _sigmoid(gb_ref[...].astype(F32)) * pb
    m_ref[...] = m.astype(m_ref.dtype)


def _merge(ya, yb, p, wa, wb, tm, tn):
    m = ya.shape[0]
    per_seg = SEG // tn
    return pl.pallas_call(
        _merge_kernel,
        out_shape=jax.ShapeDtypeStruct((m, SEG), BF16),
        grid=(per_seg, m // tm),
        in_specs=[pl.BlockSpec((tm, SEG), lambda j, i: (i, 0)),
                  pl.BlockSpec((tm, SEG), lambda j, i: (i, 0)),
                  pl.BlockSpec((tm, tn), lambda j, i: (i, SEG_GA * per_seg + j)),
                  pl.BlockSpec((tm, tn), lambda j, i: (i, SEG_GB * per_seg + j)),
                  pl.BlockSpec((SEG, tn), lambda j, i: (0, j)),
                  pl.BlockSpec((SEG, tn), lambda j, i: (0, j))],
        out_specs=pl.BlockSpec((tm, tn), lambda j, i: (i, j)),
        compiler_params=_params("arbitrary", "arbitrary"),
        name="merge",
    )(ya, yb, p, p, wa, wb)


def _out_ln_kernel(x_ref, m_ref, w_ref, g_ref, b_ref, o_ref):
    y = ALPHA * x_ref[...] + jnp.dot(m_ref[...], w_ref[...], preferred_element_type=F32)
    o_ref[...] = _layer_norm(y, g_ref[...], b_ref[...])


def _out_ln(x, m, w, g, b, tm):
    rows = x.shape[0]
    vec = pl.BlockSpec((1, D_MODEL), lambda i: (0, 0))
    return pl.pallas_call(
        _out_ln_kernel,
        out_shape=jax.ShapeDtypeStruct((rows, D_MODEL), F32),
        grid=(rows // tm,),
        in_specs=[pl.BlockSpec((tm, D_MODEL), lambda i: (i, 0)),
                  pl.BlockSpec((tm, SEG), lambda i: (i, 0)),
                  pl.BlockSpec((SEG, D_MODEL), lambda i: (0, 0)),
                  vec, vec],
        out_specs=pl.BlockSpec((tm, D_MODEL), lambda i: (i, 0)),
        compiler_params=_params("arbitrary"),
        name="out_ln",
    )(x, m, w, g, b)


def _ffn_kernel(x_ref, wu_ref, wd_ref, g_ref, b_ref, o_ref, xb_ref):
    f = pl.program_id(1)

    @pl.when(f == 0)
    def _():
        xb_ref[...] = x_ref[...].astype(BF16)

    h = jnp.dot(xb_ref[...], wu_ref[...], preferred_element_type=F32)
    h = jnp.square(jnp.maximum(h, 0.0)).astype(BF16)
    part = jnp.dot(h, wd_ref[...], preferred_element_type=F32)

    @pl.when(f == 0)
    def _():
        o_ref[...] = part

    @pl.when(f != 0)
    def _():
        o_ref[...] += part

    @pl.when(f == pl.num_programs(1) - 1)
    def _():
        o_ref[...] = _layer_norm(ALPHA * x_ref[...] + o_ref[...], g_ref[...], b_ref[...])


def _ffn(x, wu, wd, g, b, tm, tf):
    rows = x.shape[0]
    vec = pl.BlockSpec((1, D_MODEL), lambda i, f: (0, 0))
    return pl.pallas_call(
        _ffn_kernel,
        out_shape=jax.ShapeDtypeStruct((rows, D_MODEL), F32),
        grid=(rows // tm, D_FF // tf),
        in_specs=[pl.BlockSpec((tm, D_MODEL), lambda i, f: (i, 0)),
                  pl.BlockSpec((D_MODEL, tf), lambda i, f: (0, f)),
                  pl.BlockSpec((tf, D_MODEL), lambda i, f: (f, 0)),
                  vec, vec],
        out_specs=pl.BlockSpec((tm, D_MODEL), lambda i, f: (i, 0)),
        scratch_shapes=[pltpu.VMEM((tm, D_MODEL), BF16)],
        compiler_params=_params("arbitrary", "arbitrary"),
        name="ffn",
    )(x, wu, wd, g, b)


def _row_tile(m):
    return min(m, 1024)


def _lane_row(vals, offset):
    return jnp.zeros((1, BA_LANES), F32).at[0, offset:offset + vals.shape[0]].set(vals)


def _layer(x, conv_state, ssm_state, wts, batch, seq):
    (w_main, w_ba, w_s, b_s, ln_v_g, ln_v_b, w_conv, a_log, dt_bias, w_onorm, w_pa, w_pb, w_o,
     ln1_g, ln1_b, w_up, w_down, ln2_g, ln2_b) = wts
    m = x.shape[0]
    tm = _row_tile(m)
    xb = x.astype(BF16)
    p = _inproj(xb, w_main, BF16, 2 * SEG, tm, 1024)
    ba = _inproj(xb, w_ba, F32, 0, tm, BA_LANES)

    row = lambda v: v.reshape(1, -1)
    alog_row = _lane_row(a_log, HEADS)
    dt_row = _lane_row(dt_bias, HEADS)
    if conv_state is None:
        (ya,) = _mixer_a(p, w_s, b_s.T, row(ln_v_g), row(ln_v_b), emit_van=False)
        van = None
        yb, ssm_new = _delta_chunks(p, ba, w_conv, alog_row, dt_row, row(w_onorm), batch, seq)
        qkv = p[:, SEG_Q * SEG:(SEG_V + 1) * SEG].reshape(batch, seq, 3 * SEG)
        conv_new = qkv[:, seq - (CONV_W - 1):, :].astype(F32)
    else:
        w_s0 = w_s[:, 0, 0][:, None, None] * jnp.eye(A_CHUNK, dtype=F32)[None]
        b_s0 = jnp.broadcast_to(b_s[:, 0][None, :], (A_CHUNK, GROUPS))
        ya, van = _mixer_a(p, w_s0, b_s0, row(ln_v_g), row(ln_v_b), emit_van=True)
        yb, ssm_new = _delta_step(p, ba, conv_state, ssm_state, w_conv, alog_row, dt_row, row(w_onorm))
        qkv = p[:, SEG_Q * SEG:(SEG_V + 1) * SEG].astype(F32)
        conv_new = jnp.concatenate([conv_state[:, 1:, :], qkv[:, None, :]], axis=1)

    mg = _merge(ya, yb, p, w_pa, w_pb, tm, 512)
    x1 = _out_ln(x, mg, w_o, row(ln1_g), row(ln1_b), min(m, 512))
    y = _ffn(x1, w_up, w_down, row(ln2_g), row(ln2_b), min(m, 512), 512)
    return y, van, conv_new, ssm_new


def kernel(x_prompt, x_sample, state_conv, state_ssm, w_in, w_s, b_s, ln_v_g, ln_v_b, w_conv, a_log,
           dt_bias, w_onorm, w_proj_a, w_proj_b, w_o, ln1_g, ln1_b, w_up, w_down, ln2_g, ln2_b):
    depth = w_in.shape[0]
    bp, tp, _ = x_prompt.shape
    bs, ts, _ = x_sample.shape
    assert ts == 1, "the sample group advances one token per sequence"
    yp = x_prompt.reshape(bp * tp, D_MODEL)
    ys = x_sample.reshape(bs, D_MODEL)
    n_ba = 2 * HEADS
    conv_p, ssm_p, vrows_s, conv_s, ssm_s = [], [], [], [], []
    for l in range(depth):
        w_main = jnp.concatenate([w_in[l][:, :6 * SEG], w_in[l][:, 6 * SEG + n_ba:]], axis=1).astype(BF16)
        w_ba = jnp.pad(w_in[l][:, 6 * SEG:6 * SEG + n_ba], ((0, 0), (0, BA_LANES - n_ba))).astype(BF16)
        wts = (w_main, w_ba, w_s[l], b_s[l], ln_v_g[l], ln_v_b[l], w_conv[l], a_log[l], dt_bias[l],
               w_onorm[l], w_proj_a[l].astype(BF16), w_proj_b[l].astype(BF16), w_o[l].astype(BF16),
               ln1_g[l], ln1_b[l], w_up[l].astype(BF16), w_down[l].astype(BF16), ln2_g[l], ln2_b[l])
        yp, _, cp, sp = _layer(yp, None, None, wts, bp, tp)
        ys, vs, cs, ss = _layer(ys, state_conv[l], state_ssm[l], wts, bs, 1)
        conv_p.append(cp)
        ssm_p.append(sp)
        vrows_s.append(vs.reshape(bs, 1, SEG))
        conv_s.append(cs)
        ssm_s.append(ss)
    return (yp.reshape(bp, tp, D_MODEL), ys.reshape(bs, 1, D_MODEL), jnp.stack(conv_p), jnp.stack(ssm_p),
            jnp.stack(vrows_s), jnp.stack(conv_s), jnp.stack(ssm_s))
```

```python
import functools

import jax
import jax.numpy as jnp
from jax import lax
from jax.experimental import pallas as pl
from jax.experimental.pallas import tpu as pltpu

F32 = jnp.float32
BF16 = jnp.bfloat16

D_MODEL = 2048
HEADS = 16
HEAD_DIM = 128
GROUPS = 16
A_CHUNK = 128
DELTA_CHUNK = 64
CONV_W = 4
D_FF = 4 * D_MODEL
ALPHA = 2.0 ** 0.25
LN_EPS = 1e-5
RMS_EPS = 1e-6

SEG = 2048
N_MAIN = 8 * SEG
(SEG_U, SEG_VA, SEG_Q, SEG_K, SEG_V, SEG_Z, SEG_GA, SEG_GB) = range(8)
BA_LANES = 128
VMEM_LIMIT = 48 * 1024 * 1024


def _params(*sem):
    return pltpu.CompilerParams(dimension_semantics=sem, vmem_limit_bytes=VMEM_LIMIT)


def _bdot(a, b):
    return jnp.dot(a.astype(BF16), b.astype(BF16), preferred_element_type=F32)


def _gelu(x):
    return 0.5 * x * (1.0 + lax.erf(x * (2.0 ** -0.5)))


def _sigmoid(x):
    return 1.0 / (1.0 + jnp.exp(-x))


def _silu(x):
    return x * _sigmoid(x)


def _softplus(x):
    return jnp.maximum(x, 0.0) + jnp.log1p(jnp.exp(-jnp.abs(x)))


def _layer_norm(x, g, b):
    mu = jnp.mean(x, axis=-1, keepdims=True)
    xc = x - mu
    var = jnp.mean(xc * xc, axis=-1, keepdims=True)
    return xc * lax.rsqrt(var + LN_EPS) * g + b


def _inproj_kernel(x_ref, w_ref, o_ref, *, gelu_panels):
    acc = jnp.dot(x_ref[...], w_ref[...], preferred_element_type=F32)
    j = pl.program_id(0)

    @pl.when(j < gelu_panels)
    def _():
        o_ref[...] = _gelu(acc).astype(o_ref.dtype)

    @pl.when(j >= gelu_panels)
    def _():
        o_ref[...] = acc.astype(o_ref.dtype)


def _inproj(x, w, out_dtype, gelu_cols, tm, tn):
    m, k = x.shape
    n = w.shape[1]
    return pl.pallas_call(
        functools.partial(_inproj_kernel, gelu_panels=gelu_cols // tn),
        out_shape=jax.ShapeDtypeStruct((m, n), out_dtype),
        grid=(n // tn, m // tm),
        in_specs=[pl.BlockSpec((tm, k), lambda j, i: (i, 0)),
                  pl.BlockSpec((k, tn), lambda j, i: (0, j))],
        out_specs=pl.BlockSpec((tm, tn), lambda j, i: (i, j)),
        compiler_params=_params("arbitrary", "arbitrary"),
        name="inproj",
    )(x, w)


def _mixer_a_kernel(u_ref, va_ref, ws_ref, bst_ref, g_ref, b_ref, ya_ref, *van_ref):
    van = _layer_norm(va_ref[...].astype(F32), g_ref[...], b_ref[...])
    if van_ref:
        van_ref[0][...] = van
    vb = van.astype(BF16)
    row = lax.broadcasted_iota(jnp.int32, (A_CHUNK, A_CHUNK), 0)
    col = lax.broadcasted_iota(jnp.int32, (A_CHUNK, A_CHUNK), 1)
    causal = row >= col
    for g in range(GROUPS):
        sl = slice(g * A_CHUNK, (g + 1) * A_CHUNK)
        w = jnp.where(causal, ws_ref[g], 0.0).astype(BF16)
        mixed = jnp.dot(w, vb[:, sl], preferred_element_type=F32) + bst_ref[:, g:g + 1]
        ya_ref[:, sl] = (u_ref[:, sl].astype(F32) * mixed).astype(ya_ref.dtype)


def _mixer_a(p, w_s, b_s_t, ln_g, ln_b, emit_van):
    m = p.shape[0]
    out_shape = [jax.ShapeDtypeStruct((m, SEG), BF16)]
    out_specs = [pl.BlockSpec((A_CHUNK, SEG), lambda i: (i, 0))]
    if emit_van:
        out_shape.append(jax.ShapeDtypeStruct((m, SEG), F32))
        out_specs.append(pl.BlockSpec((A_CHUNK, SEG), lambda i: (i, 0)))
    return pl.pallas_call(
        _mixer_a_kernel,
        out_shape=out_shape,
        grid=(m // A_CHUNK,),
        in_specs=[pl.BlockSpec((A_CHUNK, SEG), lambda i: (i, SEG_U)),
                  pl.BlockSpec((A_CHUNK, SEG), lambda i: (i, SEG_VA)),
                  pl.BlockSpec((GROUPS, A_CHUNK, A_CHUNK), lambda i: (0, 0, 0)),
                  pl.BlockSpec((A_CHUNK, GROUPS), lambda i: (0, 0)),
                  pl.BlockSpec((1, SEG), lambda i: (0, 0)),
                  pl.BlockSpec((1, SEG), lambda i: (0, 0))],
        out_specs=out_specs,
        compiler_params=_params("arbitrary"),
        name="mixer_a",
    )(p, p, w_s, b_s_t, ln_g, ln_b)


def _beta_and_g(ba, alog_row, dt_row):
    beta = _sigmoid(ba)
    g = -jnp.exp(alog_row) * _softplus(ba + dt_row)
    return beta, g


def _gated_rms_norm(o, z, w):
    o = o * lax.rsqrt(jnp.mean(o * o, axis=-1, keepdims=True) + RMS_EPS)
    return o * w * _silu(z)


def _l2norm(x):
    return x * lax.rsqrt(jnp.sum(x * x, axis=-1, keepdims=True) + RMS_EPS)


def _unit_lower_inverse(ns, masks):
    eye, blk8, merges = masks
    nd = [jnp.where(blk8, n, 0.0) for n in ns]
    n2 = [_bdot(x, x) for x in nd]
    n4 = [_bdot(x, x) for x in n2]
    p = [eye - x for x in nd]
    p = [x + _bdot(x, y) for x, y in zip(p, n2)]
    p = [x + _bdot(x, y) for x, y in zip(p, n4)]
    for cmask in merges:
        e = [_bdot(jnp.where(cmask, n, 0.0), x) for n, x in zip(ns, p)]
        p = [x - _bdot(x, y) for x, y in zip(p, e)]
    return p


def _inverse_masks():
    c = DELTA_CHUNK
    r = lax.broadcasted_iota(jnp.int32, (c, c), 0)
    q = lax.broadcasted_iota(jnp.int32, (c, c), 1)
    eye = (r == q).astype(F32)
    blk = lambda s: (r >> s) == (q >> s)
    merges = [blk(s + 1) & jnp.logical_not(blk(s)) for s in (3, 4, 5)]
    return (eye, blk(3), merges), r >= q, r > q


def _delta_chunk_kernel(q_ref, k_ref, v_ref, z_ref, ba_ref, wq_ref, wk_ref, wv_ref, alog_ref, dt_ref,
                        won_ref, y_ref, s_out_ref, s_ref, ext_ref):
    c = pl.program_id(1)
    nc = pl.num_programs(1)
    C = DELTA_CHUNK

    @pl.when(c == 0)
    def _():
        s_ref[...] = jnp.zeros_like(s_ref)
        ext_ref[:, 0:8, :] = jnp.zeros((3, 8, SEG), F32)

    def conv_silu(x_ref, w_ref, seg):
        cur = x_ref[...].astype(F32)
        ext_ref[seg, 8:8 + C, :] = cur
        acc = cur * w_ref[3:4, :]
        for j in range(CONV_W - 1):
            acc = acc + ext_ref[seg, 5 + j:5 + j + C, :] * w_ref[j:j + 1, :]
        ext_ref[seg, 0:8, :] = cur[C - 8:C, :]
        return _silu(acc)

    qa = conv_silu(q_ref, wq_ref, 0)
    ka = conv_silu(k_ref, wk_ref, 1)
    va = conv_silu(v_ref, wv_ref, 2)

    beta, g = _beta_and_g(ba_ref[...], alog_ref[...], dt_ref[...])
    row = lax.broadcasted_iota(jnp.int32, (C, BA_LANES), 0)
    gc = g
    for s in (1, 2, 4, 8, 16, 32):
        gc = gc + jnp.where(row >= s, pltpu.roll(gc, s, axis=0), 0.0)
    gc_t = gc.T
    g_last = gc[C - 1:C, :]
    e_gc = jnp.exp(gc)
    e_rest = jnp.exp(g_last - gc)
    e_last = jnp.exp(g_last)

    inv_masks, incl, strict = _inverse_masks()
    w_on = won_ref[...]

    hs = range(HEADS)
    sl = [slice(h * HEAD_DIM, (h + 1) * HEAD_DIM) for h in hs]
    col = lambda x, h: x[:, HEADS + h:HEADS + h + 1]
    q = [_l2norm(qa[:, sl[h]]) * (HEAD_DIM ** -0.5) for h in hs]
    k = [_l2norm(ka[:, sl[h]]) for h in hs]
    kb = [k[h] * beta[:, h:h + 1] for h in hs]
    vb = [va[:, sl[h]] * beta[:, h:h + 1] for h in hs]
    decay = [jnp.exp(jnp.where(incl, col(gc, h) - gc_t[HEADS + h:HEADS + h + 1, :], -jnp.inf)) for h in hs]
    kk = [lax.dot_general(jnp.concatenate([kb[h], q[h]], axis=0).astype(BF16), k[h].astype(BF16),
                          (((1,), (1,)), ((), ())), preferred_element_type=F32) for h in hs]
    n = [jnp.where(strict, kk[h][:C] * decay[h], 0.0) for h in hs]
    attn = [kk[h][C:] * decay[h] for h in hs]
    t = _unit_lower_inverse(n, inv_masks)
    tr = [_bdot(t[h], jnp.concatenate([vb[h], kb[h] * col(e_gc, h)], axis=1)) for h in hs]
    m1 = [_bdot(jnp.concatenate([tr[h][:, HEAD_DIM:], q[h] * col(e_gc, h)], axis=0), s_ref[h]) for h in hs]
    v_new = [tr[h][:, :HEAD_DIM] - m1[h][:C] for h in hs]
    o = [m1[h][C:] + _bdot(attn[h], v_new[h]) for h in hs]
    kv = [lax.dot_general((k[h] * col(e_rest, h)).astype(BF16), v_new[h].astype(BF16),
                          (((0,), (0,)), ((), ())), preferred_element_type=F32) for h in hs]
    for h in hs:
        s_ref[h] = s_ref[h] * col(e_last, h) + kv[h]
        y_ref[:, sl[h]] = _gated_rms_norm(o[h], z_ref[:, sl[h]].astype(F32), w_on).astype(y_ref.dtype)

    @pl.when(c == nc - 1)
    def _():
        s_out_ref[0] = s_ref[...]


def _delta_chunks(p, ba, w_conv, alog_row, dt_row, w_onorm, batch, seq):
    C = DELTA_CHUNK
    nc = seq // C
    rows = lambda seg: pl.BlockSpec((C, SEG), lambda b, c: (b * nc + c, seg))
    wspec = lambda seg: pl.BlockSpec((CONV_W, SEG), lambda b, c: (0, seg))
    small = pl.BlockSpec((1, BA_LANES), lambda b, c: (0, 0))
    return pl.pallas_call(
        _delta_chunk_kernel,
        out_shape=[jax.ShapeDtypeStruct((batch * seq, SEG), BF16),
                   jax.ShapeDtypeStruct((batch, HEADS, HEAD_DIM, HEAD_DIM), F32)],
        grid=(batch, nc),
        in_specs=[rows(SEG_Q), rows(SEG_K), rows(SEG_V), rows(SEG_Z),
                  pl.BlockSpec((C, BA_LANES), lambda b, c: (b * nc + c, 0)),
                  wspec(0), wspec(1), wspec(2), small, small, small],
        out_specs=[pl.BlockSpec((C, SEG), lambda b, c: (b * nc + c, 0)),
                   pl.BlockSpec((1, HEADS, HEAD_DIM, HEAD_DIM), lambda b, c: (b, 0, 0, 0))],
        scratch_shapes=[pltpu.VMEM((HEADS, HEAD_DIM, HEAD_DIM), F32),
                        pltpu.VMEM((3, 8 + C, SEG), F32)],
        compiler_params=_params("arbitrary", "arbitrary"),
        name="delta_chunks",
    )(p, p, p, p, ba, w_conv, w_conv, w_conv, alog_row, dt_row, w_onorm)


ROWS = 8


def _delta_step_kernel(q_ref, k_ref, v_ref, z_ref, ba_ref, cs_ref, wq_ref, wk_ref, wv_ref, alog_ref,
                       dt_ref, won_ref, s_ref, y_ref, s_out_ref):
    r = pl.program_id(0) % ROWS
    rsel = lax.broadcasted_iota(jnp.int32, (ROWS, 1), 0) == r

    def pick(x):
        return jnp.sum(jnp.where(rsel, x.astype(F32), 0.0), axis=0, keepdims=True)

    def conv_silu(x_ref, w_ref, seg):
        acc = pick(x_ref[...]) * w_ref[3:4, :]
        for j in range(CONV_W - 1):
            acc = acc + cs_ref[0, j:j + 1, seg * SEG:(seg + 1) * SEG] * w_ref[j:j + 1, :]
        return _silu(acc)

    qa = conv_silu(q_ref, wq_ref, 0)
    ka = conv_silu(k_ref, wk_ref, 1)
    va = conv_silu(v_ref, wv_ref, 2)
    z = pick(z_ref[...])
    beta, g = _beta_and_g(pick(ba_ref[...]), alog_ref[...], dt_ref[...])
    e_g = jnp.exp(g)
    w_on = won_ref[...]
    pad = jnp.zeros((ROWS - 2, HEAD_DIM), F32)

    hs = range(HEADS)
    sl = [slice(h * HEAD_DIM, (h + 1) * HEAD_DIM) for h in hs]
    q = [_l2norm(qa[:, sl[h]]) * (HEAD_DIM ** -0.5) for h in hs]
    k = [_l2norm(ka[:, sl[h]]) for h in hs]
    eg = [e_g[:, HEADS + h:HEADS + h + 1] for h in hs]
    ks_qs = [_bdot(jnp.concatenate([k[h], q[h], pad], axis=0), s_ref[0, h]) for h in hs]
    v_new = [beta[:, h:h + 1] * (va[:, sl[h]] - eg[h] * ks_qs[h][0:1]) for h in hs]
    o = [eg[h] * ks_qs[h][1:2] + jnp.sum(q[h] * k[h], axis=-1, keepdims=True) * v_new[h] for h in hs]
    zero_row = jnp.zeros((1, HEAD_DIM), F32)
    kv = [lax.dot_general(jnp.concatenate([k[h], pad, zero_row], axis=0).astype(BF16),
                          jnp.broadcast_to(v_new[h], (ROWS, HEAD_DIM)).astype(BF16),
                          (((0,), (0,)), ((), ())), preferred_element_type=F32) for h in hs]
    for h in hs:
        s_out_ref[0, h] = s_ref[0, h] * eg[h] + kv[h]
    outs = [_gated_rms_norm(o[h], z[:, sl[h]], w_on) for h in hs]
    y_row = jnp.concatenate(outs, axis=1)

    @pl.when(r == 0)
    def _():
        y_ref[...] = jnp.broadcast_to(y_row, y_ref.shape).astype(y_ref.dtype)

    @pl.when(r != 0)
    def _():
        y_ref[...] = jnp.where(rsel, y_row, y_ref[...].astype(F32)).astype(y_ref.dtype)


def _delta_step(p, ba, conv_state, ssm_state, w_conv, alog_row, dt_row, w_onorm):
    n = p.shape[0]
    rows = lambda seg: pl.BlockSpec((ROWS, SEG), lambda b: (b // ROWS, seg))
    wspec = lambda seg: pl.BlockSpec((CONV_W, SEG), lambda b: (0, seg))
    small = pl.BlockSpec((1, BA_LANES), lambda b: (0, 0))
    sspec = pl.BlockSpec((1, HEADS, HEAD_DIM, HEAD_DIM), lambda b: (b, 0, 0, 0))
    return pl.pallas_call(
        _delta_step_kernel,
        out_shape=[jax.ShapeDtypeStruct((n, SEG), BF16),
                   jax.ShapeDtypeStruct(ssm_state.shape, F32)],
        grid=(n,),
        in_specs=[rows(SEG_Q), rows(SEG_K), rows(SEG_V), rows(SEG_Z),
                  pl.BlockSpec((ROWS, BA_LANES), lambda b: (b // ROWS, 0)),
                  pl.BlockSpec((1, CONV_W - 1, 3 * SEG), lambda b: (b, 0, 0)),
                  wspec(0), wspec(1), wspec(2), small, small, small, sspec],
        out_specs=[pl.BlockSpec((ROWS, SEG), lambda b: (b // ROWS, 0)), sspec],
        compiler_params=_params("arbitrary"),
        name="delta_step",
    )(p, p, p, p, ba, conv_state, w_conv, w_conv, w_conv, alog_row, dt_row, w_onorm, ssm_state)


def _merge_kernel(ya_ref, yb_ref, ga_ref, gb_ref, wa_ref, wb_ref, m_ref):
    pa = jnp.dot(ya_ref[...], wa_ref[...], preferred_element_type=F32)
    pb = jnp.dot(yb_ref[...], wb_ref[...], preferred_element_type=F32)
    m = _sigmoid(ga_ref[...].astype(F32)) * pa + _sigmoid(gb_ref[...].astype(F32)) * pb
    m_ref[...] = m.astype(m_ref.dtype)


def _merge(ya, yb, p, wa, wb, tm, tn):
    m = ya.shape[0]
    per_seg = SEG // tn
    return pl.pallas_call(
        _merge_kernel,
        out_shape=jax.ShapeDtypeStruct((m, SEG), BF16),
        grid=(per_seg, m // tm),
        in_specs=[pl.BlockSpec((tm, SEG), lambda j, i: (i, 0)),
                  pl.BlockSpec((tm, SEG), lambda j, i: (i, 0)),
                  pl.BlockSpec((tm, tn), lambda j, i: (i, SEG_GA * per_seg + j)),
                  pl.BlockSpec((tm, tn), lambda j, i: (i, SEG_GB * per_seg + j)),
                  pl.BlockSpec((SEG, tn), lambda j, i: (0, j)),
                  pl.BlockSpec((SEG, tn), lambda j, i: (0, j))],
        out_specs=pl.BlockSpec((tm, tn), lambda j, i: (i, j)),
        compiler_params=_params("arbitrary", "arbitrary"),
        name="merge",
    )(ya, yb, p, p, wa, wb)


def _out_ln_kernel(x_ref, m_ref, w_ref, g_ref, b_ref, o_ref):
    y = ALPHA * x_ref[...] + jnp.dot(m_ref[...], w_ref[...], preferred_element_type=F32)
    o_ref[...] = _layer_norm(y, g_ref[...], b_ref[...])


def _out_ln(x, m, w, g, b, tm):
    rows = x.shape[0]
    vec = pl.BlockSpec((1, D_MODEL), lambda i: (0, 0))
    return pl.pallas_call(
        _out_ln_kernel,
        out_shape=jax.ShapeDtypeStruct((rows, D_MODEL), F32),
        grid=(rows // tm,),
        in_specs=[pl.BlockSpec((tm, D_MODEL), lambda i: (i, 0)),
                  pl.BlockSpec((tm, SEG), lambda i: (i, 0)),
                  pl.BlockSpec((SEG, D_MODEL), lambda i: (0, 0)),
                  vec, vec],
        out_specs=pl.BlockSpec((tm, D_MODEL), lambda i: (i, 0)),
        compiler_params=_params("arbitrary"),
        name="out_ln",
    )(x, m, w, g, b)


def _ffn_kernel(x_ref, wu_ref, wd_ref, g_ref, b_ref, o_ref, xb_ref):
    f = pl.program_id(1)

    @pl.when(f == 0)
    def _():
        xb_ref[...] = x_ref[...].astype(BF16)

    h = jnp.dot(xb_ref[...], wu_ref[...], preferred_element_type=F32)
    h = jnp.square(jnp.maximum(h, 0.0)).astype(BF16)
    part = jnp.dot(h, wd_ref[...], preferred_element_type=F32)

    @pl.when(f == 0)
    def _():
        o_ref[...] = part

    @pl.when(f != 0)
    def _():
        o_ref[...] += part

    @pl.when(f == pl.num_programs(1) - 1)
    def _():
        o_ref[...] = _layer_norm(ALPHA * x_ref[...] + o_ref[...], g_ref[...], b_ref[...])


def _ffn(x, wu, wd, g, b, tm, tf):
    rows = x.shape[0]
    vec = pl.BlockSpec((1, D_MODEL), lambda i, f: (0, 0))
    return pl.pallas_call(
        _ffn_kernel,
        out_shape=jax.ShapeDtypeStruct((rows, D_MODEL), F32),
        grid=(rows // tm, D_FF // tf),
        in_specs=[pl.BlockSpec((tm, D_MODEL), lambda i, f: (i, 0)),
                  pl.BlockSpec((D_MODEL, tf), lambda i, f: (0, f)),
                  pl.BlockSpec((tf, D_MODEL), lambda i, f: (f, 0)),
                  vec, vec],
        out_specs=pl.BlockSpec((tm, D_MODEL), lambda i, f: (i, 0)),
        scratch_shapes=[pltpu.VMEM((tm, D_MODEL), BF16)],
        compiler_params=_params("arbitrary", "arbitrary"),
        name="ffn",
    )(x, wu, wd, g, b)


def _row_tile(m):
    return min(m, 1024)


def _lane_row(vals, offset):
    return jnp.zeros((1, BA_LANES), F32).at[0, offset:offset + vals.shape[0]].set(vals)


def _layer(x, conv_state, ssm_state, wts, batch, seq):
    (w_main, w_ba, w_s, b_s, ln_v_g, ln_v_b, w_conv, a_log, dt_bias, w_onorm, w_pa, w_pb, w_o,
     ln1_g, ln1_b, w_up, w_down, ln2_g, ln2_b) = wts
    m = x.shape[0]
    tm = _row_tile(m)
    xb = x.astype(BF16)
    p = _inproj(xb, w_main, BF16, 2 * SEG, tm, 1024)
    ba = _inproj(xb, w_ba, F32, 0, tm, BA_LANES)

    row = lambda v: v.reshape(1, -1)
    alog_row = _lane_row(a_log, HEADS)
    dt_row = _lane_row(dt_bias, HEADS)
    if conv_state is None:
        (ya,) = _mixer_a(p, w_s, b_s.T, row(ln_v_g), row(ln_v_b), emit_van=False)
        van = None
        yb, ssm_new = _delta_chunks(p, ba, w_conv, alog_row, dt_row, row(w_onorm), batch, seq)
        qkv = p[:, SEG_Q * SEG:(SEG_V + 1) * SEG].reshape(batch, seq, 3 * SEG)
        conv_new = qkv[:, seq - (CONV_W - 1):, :].astype(F32)
    else:
        w_s0 = w_s[:, 0, 0][:, None, None] * jnp.eye(A_CHUNK, dtype=F32)[None]
        b_s0 = jnp.broadcast_to(b_s[:, 0][None, :], (A_CHUNK, GROUPS))
        ya, van = _mixer_a(p, w_s0, b_s0, row(ln_v_g), row(ln_v_b), emit_van=True)
        yb, ssm_new = _delta_step(p, ba, conv_state, ssm_state, w_conv, alog_row, dt_row, row(w_onorm))
        qkv = p[:, SEG_Q * SEG:(SEG_V + 1) * SEG].astype(F32)
        conv_new = jnp.concatenate([conv_state[:, 1:, :], qkv[:, None, :]], axis=1)

    mg = _merge(ya, yb, p, w_pa, w_pb, tm, 512)
    x1 = _out_ln(x, mg, w_o, row(ln1_g), row(ln1_b), min(m, 512))
    y = _ffn(x1, w_up, w_down, row(ln2_g), row(ln2_b), min(m, 512), 512)
    return y, van, conv_new, ssm_new


def kernel(x_prompt, x_sample, state_conv, state_ssm, w_in, w_s, b_s, ln_v_g, ln_v_b, w_conv, a_log,
           dt_bias, w_onorm, w_proj_a, w_proj_b, w_o, ln1_g, ln1_b, w_up, w_down, ln2_g, ln2_b):
    depth = w_in.shape[0]
    bp, tp, _ = x_prompt.shape
    bs, ts, _ = x_sample.shape
    assert ts == 1, "the sample group advances one token per sequence"
    yp = x_prompt.reshape(bp * tp, D_MODEL)
    ys = x_sample.reshape(bs, D_MODEL)
    n_ba = 2 * HEADS
    conv_p, ssm_p, vrows_s, conv_s, ssm_s = [], [], [], [], []
    for l in range(depth):
        w_main = jnp.concatenate([w_in[l][:, :6 * SEG], w_in[l][:, 6 * SEG + n_ba:]], axis=1).astype(BF16)
        w_ba = jnp.pad(w_in[l][:, 6 * SEG:6 * SEG + n_ba], ((0, 0), (0, BA_LANES - n_ba))).astype(BF16)
        wts = (w_main, w_ba, w_s[l], b_s[l], ln_v_g[l], ln_v_b[l], w_conv[l], a_log[l], dt_bias[l],
               w_onorm[l], w_proj_a[l].astype(BF16), w_proj_b[l].astype(BF16), w_o[l].astype(BF16),
               ln1_g[l], ln1_b[l], w_up[l].astype(BF16), w_down[l].astype(BF16), ln2_g[l], ln2_b[l])
        yp, _, cp, sp = _layer(yp, None, None, wts, bp, tp)
        ys, vs, cs, ss = _layer(ys, state_conv[l], state_ssm[l], wts, bs, 1)
        conv_p.append(cp)
        ssm_p.append(sp)
        vrows_s.append(vs.reshape(bs, 1, SEG))
        conv_s.append(cs)
        ssm_s.append(ss)
    return (yp.reshape(bp, tp, D_MODEL), ys.reshape(bs, 1, D_MODEL), jnp.stack(conv_p), jnp.stack(ssm_p),
            jnp.stack(vrows_s), jnp.stack(conv_s), jnp.stack(ssm_s))
```

```python
import functools

import jax
import jax.numpy as jnp
from jax import lax
from jax.experimental import pallas as pl
from jax.experimental.pallas import tpu as pltpu

F32 = jnp.float32
BF16 = jnp.bfloat16

D_MODEL = 2048
HEADS = 16
HEAD_DIM = 128
GROUPS = 16
A_CHUNK = 128
DELTA_CHUNK = 64
CONV_W = 4
D_FF = 4 * D_MODEL
ALPHA = 2.0 ** 0.25
LN_EPS = 1e-5
RMS_EPS = 1e-6

SEG = 2048
N_MAIN = 6 * SEG
(SEG_U, SEG_VA, SEG_Q, SEG_K, SEG_V, SEG_Z) = range(6)
N_BA = 2 * HEADS
(SEG_GA, SEG_GB) = range(2)
CARRY = 16
BA_LANES = 128
VMEM_LIMIT = 48 * 1024 * 1024
FFN_VMEM_LIMIT = 56 * 1024 * 1024


def _params(*sem):
    return pltpu.CompilerParams(dimension_semantics=sem, vmem_limit_bytes=VMEM_LIMIT)


def _bdot(a, b):
    return jnp.dot(a.astype(BF16), b.astype(BF16), preferred_element_type=F32)


def _gelu(x):
    return 0.5 * x * (1.0 + lax.erf(x * (2.0 ** -0.5)))


def _sigmoid(x):
    return 1.0 / (1.0 + jnp.exp(-x))


def _silu(x):
    return x * _sigmoid(x)


def _softplus(x):
    return jnp.maximum(x, 0.0) + jnp.log1p(jnp.exp(-jnp.abs(x)))


def _layer_norm(x, g, b):
    mu = jnp.mean(x, axis=-1, keepdims=True)
    xc = x - mu
    var = jnp.mean(xc * xc, axis=-1, keepdims=True)
    return xc * lax.rsqrt(var + LN_EPS) * g + b


def _inproj_kernel(x_ref, w_ref, o_ref, *wb_ref, gelu_panels):
    if wb_ref:
        @pl.when(pl.program_id(1) == 0)
        def _():
            wb_ref[0][...] = w_ref[...].astype(BF16)
        w = wb_ref[0][...]
    else:
        w = w_ref[...]
    acc = jnp.dot(x_ref[...], w, preferred_element_type=F32)
    j = pl.program_id(0)

    @pl.when(j < gelu_panels)
    def _():
        o_ref[...] = _gelu(acc).astype(o_ref.dtype)

    @pl.when(j >= gelu_panels)
    def _():
        o_ref[...] = acc.astype(o_ref.dtype)


def _inproj(x, w, n, out_dtype, gelu_cols, tm, tn):
    m, k = x.shape
    cast = w.dtype != BF16
    out_shape = [jax.ShapeDtypeStruct((m, n), out_dtype)]
    out_specs = [pl.BlockSpec((tm, tn), lambda j, i: (i, j))]
    if cast:
        out_shape.append(jax.ShapeDtypeStruct((k, n), BF16))
        out_specs.append(pl.BlockSpec((k, tn), lambda j, i: (0, j)))
    res = pl.pallas_call(
        functools.partial(_inproj_kernel, gelu_panels=gelu_cols // tn),
        out_shape=out_shape,
        grid=(n // tn, m // tm),
        in_specs=[pl.BlockSpec((tm, k), lambda j, i: (i, 0)),
                  pl.BlockSpec((k, tn), lambda j, i: (0, j))],
        out_specs=out_specs,
        compiler_params=_params("arbitrary", "arbitrary"),
        name="inproj",
    )(x, w)
    return (res[0], res[1]) if cast else (res[0], w)


def _mixer_a_kernel(u_ref, va_ref, ws_ref, bst_ref, g_ref, b_ref, ya_ref, *van_ref):
    van = _layer_norm(va_ref[...].astype(F32), g_ref[...], b_ref[...])
    if van_ref:
        van_ref[0][...] = van
    vb = van.astype(BF16)
    row = lax.broadcasted_iota(jnp.int32, (A_CHUNK, A_CHUNK), 0)
    col = lax.broadcasted_iota(jnp.int32, (A_CHUNK, A_CHUNK), 1)
    causal = row >= col
    for g in range(GROUPS):
        sl = slice(g * A_CHUNK, (g + 1) * A_CHUNK)
        w = jnp.where(causal, ws_ref[g], 0.0).astype(BF16)
        mixed = jnp.dot(w, vb[:, sl], preferred_element_type=F32) + bst_ref[:, g:g + 1]
        ya_ref[:, sl] = (u_ref[:, sl].astype(F32) * mixed).astype(ya_ref.dtype)


def _mixer_a(p, w_s, b_s_t, ln_g, ln_b, emit_van):
    m = p.shape[0]
    out_shape = [jax.ShapeDtypeStruct((m, SEG), BF16)]
    out_specs = [pl.BlockSpec((A_CHUNK, SEG), lambda i: (i, 0))]
    if emit_van:
        out_shape.append(jax.ShapeDtypeStruct((m, SEG), F32))
        out_specs.append(pl.BlockSpec((A_CHUNK, SEG), lambda i: (i, 0)))
    return pl.pallas_call(
        _mixer_a_kernel,
        out_shape=out_shape,
        grid=(m // A_CHUNK,),
        in_specs=[pl.BlockSpec((A_CHUNK, SEG), lambda i: (i, SEG_U)),
                  pl.BlockSpec((A_CHUNK, SEG), lambda i: (i, SEG_VA)),
                  pl.BlockSpec((GROUPS, A_CHUNK, A_CHUNK), lambda i: (0, 0, 0)),
                  pl.BlockSpec((A_CHUNK, GROUPS), lambda i: (0, 0)),
                  pl.BlockSpec((1, SEG), lambda i: (0, 0)),
                  pl.BlockSpec((1, SEG), lambda i: (0, 0))],
        out_specs=out_specs,
        compiler_params=_params("arbitrary"),
        name="mixer_a",
    )(p, p, w_s, b_s_t, ln_g, ln_b)


def _beta_and_g(ba, alog_row, dt_row):
    beta = _sigmoid(ba)
    g = -jnp.exp(alog_row) * _softplus(ba + dt_row)
    return beta, g


def _gated_rms_norm(o, z, w):
    o = o * lax.rsqrt(jnp.mean(o * o, axis=-1, keepdims=True) + RMS_EPS)
    return o * w * _silu(z)


def _l2norm(x):
    return x * lax.rsqrt(jnp.sum(x * x, axis=-1, keepdims=True) + RMS_EPS)


def _unit_lower_inverse(ns, masks):
    eye, blk8, merges = masks
    nd = [jnp.where(blk8, n, 0.0) for n in ns]
    n2 = [_bdot(x, x) for x in nd]
    n4 = [_bdot(x, x) for x in n2]
    p = [eye - x for x in nd]
    p = [x + _bdot(x, y) for x, y in zip(p, n2)]
    p = [x + _bdot(x, y) for x, y in zip(p, n4)]
    for cmask in merges:
        e = [_bdot(jnp.where(cmask, n, 0.0), x) for n, x in zip(ns, p)]
        p = [x - _bdot(x, y) for x, y in zip(p, e)]
    return p


def _inverse_masks():
    c = DELTA_CHUNK
    r = lax.broadcasted_iota(jnp.int32, (c, c), 0)
    q = lax.broadcasted_iota(jnp.int32, (c, c), 1)
    eye = (r == q).astype(F32)
    blk = lambda s: (r >> s) == (q >> s)
    merges = [blk(s + 1) & jnp.logical_not(blk(s)) for s in (3, 4, 5)]
    return (eye, blk(3), merges), r >= q, r > q


def _delta_chunk_kernel(q_ref, k_ref, v_ref, z_ref, ba_ref, wq_ref, wk_ref, wv_ref, alog_ref, dt_ref,
                        won_ref, y_ref, s_out_ref, s_ref, carry_ref):
    c = pl.program_id(1)
    nc = pl.num_programs(1)
    C = DELTA_CHUNK

    @pl.when(c == 0)
    def _():
        s_ref[...] = jnp.zeros_like(s_ref)
        carry_ref[...] = jnp.zeros_like(carry_ref)

    sr = lax.broadcasted_iota(jnp.int32, ((CONV_W - 1) * C, CARRY + C), 0)
    sc = lax.broadcasted_iota(jnp.int32, ((CONV_W - 1) * C, CARRY + C), 1)
    tap = sr >> (C.bit_length() - 1)
    shift = jnp.where(sc - (sr & (C - 1)) - tap == CARRY - (CONV_W - 1), 1.0, 0.0).astype(BF16)

    def conv_silu(x_ref, w_ref, seg):
        cur = x_ref[...]
        ext = jnp.concatenate([carry_ref[seg], cur], axis=0)
        sh = jnp.dot(shift, ext, preferred_element_type=F32)
        acc = cur.astype(F32) * w_ref[CONV_W - 1:CONV_W, :]
        for j in range(CONV_W - 1):
            acc = acc + sh[j * C:(j + 1) * C, :] * w_ref[j:j + 1, :]
        carry_ref[seg] = cur[C - CARRY:C, :]
        return _silu(acc)

    qa = conv_silu(q_ref, wq_ref, 0)
    ka = conv_silu(k_ref, wk_ref, 1)
    va = conv_silu(v_ref, wv_ref, 2)

    beta, g = _beta_and_g(ba_ref[...], alog_ref[...], dt_ref[...])
    row = lax.broadcasted_iota(jnp.int32, (C, BA_LANES), 0)
    gc = g
    for s in (1, 2, 4, 8, 16, 32):
        gc = gc + jnp.where(row >= s, pltpu.roll(gc, s, axis=0), 0.0)
    gc_t = gc.T
    g_last = gc[C - 1:C, :]
    e_gc = jnp.exp(gc)
    e_rest = jnp.exp(g_last - gc)
    e_last = jnp.exp(g_last)

    inv_masks, incl, strict = _inverse_masks()
    w_on = won_ref[...]

    hs = range(HEADS)
    sl = [slice(h * HEAD_DIM, (h + 1) * HEAD_DIM) for h in hs]
    col = lambda x, h: x[:, HEADS + h:HEADS + h + 1]
    q = [_l2norm(qa[:, sl[h]]) * (HEAD_DIM ** -0.5) for h in hs]
    k = [_l2norm(ka[:, sl[h]]) for h in hs]
    kb = [k[h] * beta[:, h:h + 1] for h in hs]
    vb = [va[:, sl[h]] * beta[:, h:h + 1] for h in hs]
    decay = [jnp.exp(jnp.where(incl, col(gc, h) - gc_t[HEADS + h:HEADS + h + 1, :], -jnp.inf)) for h in hs]
    kk = [lax.dot_general(jnp.concatenate([kb[h], q[h]], axis=0).astype(BF16), k[h].astype(BF16),
                          (((1,), (1,)), ((), ())), preferred_element_type=F32) for h in hs]
    n = [jnp.where(strict, kk[h][:C] * decay[h], 0.0) for h in hs]
    attn = [kk[h][C:] * decay[h] for h in hs]
    t = _unit_lower_inverse(n, inv_masks)
    tr = [_bdot(t[h], jnp.concatenate([vb[h], kb[h] * col(e_gc, h)], axis=1)) for h in hs]
    m1 = [_bdot(jnp.concatenate([tr[h][:, HEAD_DIM:], q[h] * col(e_gc, h)], axis=0), s_ref[h]) for h in hs]
    v_new = [tr[h][:, :HEAD_DIM] - m1[h][:C] for h in hs]
    o = [m1[h][C:] + _bdot(attn[h], v_new[h]) for h in hs]
    kv = [lax.dot_general((k[h] * col(e_rest, h)).astype(BF16), v_new[h].astype(BF16),
                          (((0,), (0,)), ((), ())), preferred_element_type=F32) for h in hs]
    for h in hs:
        s_ref[h] = s_ref[h] * col(e_last, h) + kv[h]
        y_ref[:, sl[h]] = _gated_rms_norm(o[h], z_ref[:, sl[h]].astype(F32), w_on).astype(y_ref.dtype)

    @pl.when(c == nc - 1)
    def _():
        s_out_ref[0] = s_ref[...]


def _delta_chunks(p, ba, w_conv, alog_row, dt_row, w_onorm, batch, seq):
    C = DELTA_CHUNK
    nc = seq // C
    rows = lambda seg: pl.BlockSpec((C, SEG), lambda b, c: (b * nc + c, seg))
    wspec = lambda seg: pl.BlockSpec((CONV_W, SEG), lambda b, c: (0, seg))
    small = pl.BlockSpec((1, BA_LANES), lambda b, c: (0, 0))
    return pl.pallas_call(
        _delta_chunk_kernel,
        out_shape=[jax.ShapeDtypeStruct((batch * seq, SEG), BF16),
                   jax.ShapeDtypeStruct((batch, HEADS, HEAD_DIM, HEAD_DIM), F32)],
        grid=(batch, nc),
        in_specs=[rows(SEG_Q), rows(SEG_K), rows(SEG_V), rows(SEG_Z),
                  pl.BlockSpec((C, BA_LANES), lambda b, c: (b * nc + c, 0)),
                  wspec(0), wspec(1), wspec(2), small, small, small],
        out_specs=[pl.BlockSpec((C, SEG), lambda b, c: (b * nc + c, 0)),
                   pl.BlockSpec((1, HEADS, HEAD_DIM, HEAD_DIM), lambda b, c: (b, 0, 0, 0))],
        scratch_shapes=[pltpu.VMEM((HEADS, HEAD_DIM, HEAD_DIM), F32),
                        pltpu.VMEM((3, CARRY, SEG), BF16)],
        compiler_params=_params("arbitrary", "arbitrary"),
        name="delta_chunks",
    )(p, p, p, p, ba, w_conv, w_conv, w_conv, alog_row, dt_row, w_onorm)


ROWS = 8


def _delta_step_kernel(q_ref, k_ref, v_ref, z_ref, ba_ref, cs_ref, wq_ref, wk_ref, wv_ref, alog_ref,
                       dt_ref, won_ref, s_ref, y_ref, s_out_ref):
    r = pl.program_id(0) % ROWS
    rsel = lax.broadcasted_iota(jnp.int32, (ROWS, 1), 0) == r

    def pick(x):
        return jnp.sum(jnp.where(rsel, x.astype(F32), 0.0), axis=0, keepdims=True)

    def conv_silu(x_ref, w_ref, seg):
        acc = pick(x_ref[...]) * w_ref[3:4, :]
        for j in range(CONV_W - 1):
            acc = acc + cs_ref[0, j:j + 1, seg * SEG:(seg + 1) * SEG] * w_ref[j:j + 1, :]
        return _silu(acc)

    qa = conv_silu(q_ref, wq_ref, 0)
    ka = conv_silu(k_ref, wk_ref, 1)
    va = conv_silu(v_ref, wv_ref, 2)
    z = pick(z_ref[...])
    beta, g = _beta_and_g(pick(ba_ref[...]), alog_ref[...], dt_ref[...])
    e_g = jnp.exp(g)
    w_on = won_ref[...]
    pad = jnp.zeros((ROWS - 2, HEAD_DIM), F32)

    hs = range(HEADS)
    sl = [slice(h * HEAD_DIM, (h + 1) * HEAD_DIM) for h in hs]
    q = [_l2norm(qa[:, sl[h]]) * (HEAD_DIM ** -0.5) for h in hs]
    k = [_l2norm(ka[:, sl[h]]) for h in hs]
    eg = [e_g[:, HEADS + h:HEADS + h + 1] for h in hs]
    ks_qs = [_bdot(jnp.concatenate([k[h], q[h], pad], axis=0), s_ref[0, h]) for h in hs]
    v_new = [beta[:, h:h + 1] * (va[:, sl[h]] - eg[h] * ks_qs[h][0:1]) for h in hs]
    o = [eg[h] * ks_qs[h][1:2] + jnp.sum(q[h] * k[h], axis=-1, keepdims=True) * v_new[h] for h in hs]
    zero_row = jnp.zeros((1, HEAD_DIM), F32)
    kv = [lax.dot_general(jnp.concatenate([k[h], pad, zero_row], axis=0).astype(BF16),
                          jnp.broadcast_to(v_new[h], (ROWS, HEAD_DIM)).astype(BF16),
                          (((0,), (0,)), ((), ())), preferred_element_type=F32) for h in hs]
    for h in hs:
        s_out_ref[0, h] = s_ref[0, h] * eg[h] + kv[h]
    outs = [_gated_rms_norm(o[h], z[:, sl[h]], w_on) for h in hs]
    y_row = jnp.concatenate(outs, axis=1)

    @pl.when(r == 0)
    def _():
        y_ref[...] = jnp.broadcast_to(y_row, y_ref.shape).astype(y_ref.dtype)

    @pl.when(r != 0)
    def _():
        y_ref[...] = jnp.where(rsel, y_row, y_ref[...].astype(F32)).astype(y_ref.dtype)


def _delta_step(p, ba, conv_state, ssm_state, w_conv, alog_row, dt_row, w_onorm):
    n = p.shape[0]
    rows = lambda seg: pl.BlockSpec((ROWS, SEG), lambda b: (b // ROWS, seg))
    wspec = lambda seg: pl.BlockSpec((CONV_W, SEG), lambda b: (0, seg))
    small = pl.BlockSpec((1, BA_LANES), lambda b: (0, 0))
    sspec = pl.BlockSpec((1, HEADS, HEAD_DIM, HEAD_DIM), lambda b: (b, 0, 0, 0))
    return pl.pallas_call(
        _delta_step_kernel,
        out_shape=[jax.ShapeDtypeStruct((n, SEG), BF16),
                   jax.ShapeDtypeStruct(ssm_state.shape, F32)],
        grid=(n,),
        in_specs=[rows(SEG_Q), rows(SEG_K), rows(SEG_V), rows(SEG_Z),
                  pl.BlockSpec((ROWS, BA_LANES), lambda b: (b // ROWS, 0)),
                  pl.BlockSpec((1, CONV_W - 1, 3 * SEG), lambda b: (b, 0, 0)),
                  wspec(0), wspec(1), wspec(2), small, small, small, sspec],
        out_specs=[pl.BlockSpec((ROWS, SEG), lambda b: (b // ROWS, 0)), sspec],
        compiler_params=_params("arbitrary"),
        name="delta_step",
    )(p, p, p, p, ba, conv_state, w_conv, w_conv, w_conv, alog_row, dt_row, w_onorm, ssm_state)


def _merge_kernel(ya_ref, yb_ref, ga_ref, gb_ref, wa_ref, wb_ref, m_ref):
    pa = jnp.dot(ya_ref[...], wa_ref[...], preferred_element_type=F32)
    pb = jnp.dot(yb_ref[...], wb_ref[...], preferred_element_type=F32)
    m = _sigmoid(ga_ref[...].astype(F32)) * pa + _sigmoid(gb_ref[...].astype(F32)) * pb
    m_ref[...] = m.astype(m_ref.dtype)


def _merge(ya, yb, gates, wa, wb, tm, tn):
    m = ya.shape[0]
    per_seg = SEG // tn
    return pl.pallas_call(
        _merge_kernel,
        out_shape=jax.ShapeDtypeStruct((m, SEG), BF16),
        grid=(per_seg, m // tm),
        in_specs=[pl.BlockSpec((tm, SEG), lambda j, i: (i, 0)),
                  pl.BlockSpec((tm, SEG), lambda j, i: (i, 0)),
                  pl.BlockSpec((tm, tn), lambda j, i: (i, SEG_GA * per_seg + j)),
                  pl.BlockSpec((tm, tn), lambda j, i: (i, SEG_GB * per_seg + j)),
                  pl.BlockSpec((SEG, tn), lambda j, i: (0, j)),
                  pl.BlockSpec((SEG, tn), lambda j, i: (0, j))],
        out_specs=pl.BlockSpec((tm, tn), lambda j, i: (i, j)),
        compiler_params=_params("arbitrary", "arbitrary"),
        name="merge",
    )(ya, yb, gates, gates, wa, wb)


def _out_ln_kernel(x_ref, m_ref, w_ref, g_ref, b_ref, o_ref):
    y = ALPHA * x_ref[...] + jnp.dot(m_ref[...], w_ref[...], preferred_element_type=F32)
    o_ref[...] = _layer_norm(y, g_ref[...], b_ref[...])


def _out_ln(x, m, w, g, b, tm):
    rows = x.shape[0]
    vec = pl.BlockSpec((1, D_MODEL), lambda i: (0, 0))
    return pl.pallas_call(
        _out_ln_kernel,
        out_shape=jax.ShapeDtypeStruct((rows, D_MODEL), F32),
        grid=(rows // tm,),
        in_specs=[pl.BlockSpec((tm, D_MODEL), lambda i: (i, 0)),
                  pl.BlockSpec((tm, SEG), lambda i: (i, 0)),
                  pl.BlockSpec((SEG, D_MODEL), lambda i: (0, 0)),
                  vec, vec],
        out_specs=pl.BlockSpec((tm, D_MODEL), lambda i: (i, 0)),
        compiler_params=_params("arbitrary"),
        name="out_ln",
    )(x, m, w, g, b)


def _ffn_kernel(x_ref, wu_ref, wd_ref, g_ref, b_ref, o_ref, xb_ref):
    f = pl.program_id(1)

    @pl.when(f == 0)
    def _():
        xb_ref[...] = x_ref[...].astype(BF16)

    h = jnp.dot(xb_ref[...], wu_ref[...], preferred_element_type=F32)
    h = jnp.square(jnp.maximum(h, 0.0)).astype(BF16)
    part = jnp.dot(h, wd_ref[...], preferred_element_type=F32)

    @pl.when(f == 0)
    def _():
        o_ref[...] = part

    @pl.when(f != 0)
    def _():
        o_ref[...] += part

    @pl.when(f == pl.num_programs(1) - 1)
    def _():
        o_ref[...] = _layer_norm(ALPHA * x_ref[...] + o_ref[...], g_ref[...], b_ref[...])


def _ffn(x, wu, wd, g, b, tm, tf):
    rows = x.shape[0]
    vec = pl.BlockSpec((1, D_MODEL), lambda i, f: (0, 0))
    return pl.pallas_call(
        _ffn_kernel,
        out_shape=jax.ShapeDtypeStruct((rows, D_MODEL), F32),
        grid=(rows // tm, D_FF // tf),
        in_specs=[pl.BlockSpec((tm, D_MODEL), lambda i, f: (i, 0), pipeline_mode=pl.Buffered(1)),
                  pl.BlockSpec((D_MODEL, tf), lambda i, f: (0, f)),
                  pl.BlockSpec((tf, D_MODEL), lambda i, f: (f, 0)),
                  vec, vec],
        out_specs=pl.BlockSpec((tm, D_MODEL), lambda i, f: (i, 0)),
        scratch_shapes=[pltpu.VMEM((tm, D_MODEL), BF16)],
        compiler_params=pltpu.CompilerParams(dimension_semantics=("arbitrary", "arbitrary"),
                                             vmem_limit_bytes=FFN_VMEM_LIMIT),
        name="ffn",
    )(x, wu, wd, g, b)


def _row_tile(m):
    return min(m, 1024)


def _lane_row(vals, offset):
    return jnp.zeros((1, BA_LANES), F32).at[0, offset:offset + vals.shape[0]].set(vals)


def _layer(x, conv_state, ssm_state, wts, batch, seq):
    (w_main, w_gates, w_ba, w_s, b_s, ln_v_g, ln_v_b, w_conv, a_log, dt_bias, w_onorm, w_pa, w_pb, w_o,
     ln1_g, ln1_b, w_up, w_down, ln2_g, ln2_b) = wts
    m = x.shape[0]
    tm = _row_tile(m)
    xb = x.astype(BF16)
    p, w_main = _inproj(xb, w_main, N_MAIN, BF16, 2 * SEG, tm, 1024)
    gates, w_gates = _inproj(xb, w_gates, 2 * SEG, BF16, 0, tm, 1024)
    ba, w_ba = _inproj(xb, w_ba, BA_LANES, F32, 0, tm, BA_LANES)

    row = lambda v: v.reshape(1, -1)
    alog_row = _lane_row(a_log, HEADS)
    dt_row = _lane_row(dt_bias, HEADS)
    if conv_state is None:
        (ya,) = _mixer_a(p, w_s, b_s.T, row(ln_v_g), row(ln_v_b), emit_van=False)
        van = None
        yb, ssm_new = _delta_chunks(p, ba, w_conv, alog_row, dt_row, row(w_onorm), batch, seq)
        tail = p.reshape(batch, seq, N_MAIN)[:, seq - (CONV_W - 1):, SEG_Q * SEG:(SEG_V + 1) * SEG]
        conv_new = tail.astype(F32)
    else:
        w_s0 = w_s[:, 0, 0][:, None, None] * jnp.eye(A_CHUNK, dtype=F32)[None]
        b_s0 = jnp.broadcast_to(b_s[:, 0][None, :], (A_CHUNK, GROUPS))
        ya, van = _mixer_a(p, w_s0, b_s0, row(ln_v_g), row(ln_v_b), emit_van=True)
        yb, ssm_new = _delta_step(p, ba, conv_state, ssm_state, w_conv, alog_row, dt_row, row(w_onorm))
        qkv = p[:, SEG_Q * SEG:(SEG_V + 1) * SEG].astype(F32)
        conv_new = jnp.concatenate([conv_state[:, 1:, :], qkv[:, None, :]], axis=1)

    mg = _merge(ya, yb, gates, w_pa, w_pb, tm, 512)
    x1 = _out_ln(x, mg, w_o, row(ln1_g), row(ln1_b), min(m, 512))
    y = _ffn(x1, w_up, w_down, row(ln2_g), row(ln2_b), tm, 512)
    wts = (w_main, w_gates, w_ba) + tuple(wts[3:])
    return (y, van, conv_new, ssm_new), wts


def kernel(x_prompt, x_sample, state_conv, state_ssm, w_in, w_s, b_s, ln_v_g, ln_v_b, w_conv, a_log,
           dt_bias, w_onorm, w_proj_a, w_proj_b, w_o, ln1_g, ln1_b, w_up, w_down, ln2_g, ln2_b):
    depth = w_in.shape[0]
    bp, tp, _ = x_prompt.shape
    bs, ts, _ = x_sample.shape
    assert ts == 1, "the sample group advances one token per sequence"
    yp = x_prompt.reshape(bp * tp, D_MODEL)
    ys = x_sample.reshape(bs, D_MODEL)
    conv_p, ssm_p, vrows_s, conv_s, ssm_s = [], [], [], [], []
    for l in range(depth):
        w_gates = w_in[l][:, N_MAIN + N_BA:]
        w_ba = jnp.pad(w_in[l][:, N_MAIN:N_MAIN + N_BA], ((0, 0), (0, BA_LANES - N_BA)))
        wts = (w_in[l], w_gates, w_ba, w_s[l], b_s[l], ln_v_g[l], ln_v_b[l], w_conv[l], a_log[l], dt_bias[l],
               w_onorm[l], w_proj_a[l].astype(BF16), w_proj_b[l].astype(BF16), w_o[l].astype(BF16),
               ln1_g[l], ln1_b[l], w_up[l].astype(BF16), w_down[l].astype(BF16), ln2_g[l], ln2_b[l])
        (yp, _, cp, sp), wts = _layer(yp, None, None, wts, bp, tp)
        (ys, vs, cs, ss), _ = _layer(ys, state_conv[l], state_ssm[l], wts, bs, 1)
        conv_p.append(cp)
        ssm_p.append(sp)
        vrows_s.append(vs.reshape(bs, 1, SEG))
        conv_s.append(cs)
        ssm_s.append(ss)
    return (yp.reshape(bp, tp, D_MODEL), ys.reshape(bs, 1, D_MODEL), jnp.stack(conv_p), jnp.stack(ssm_p),
            jnp.stack(vrows_s), jnp.stack(conv_s), jnp.stack(ssm_s))
```

```python
import functools

import jax
import jax.numpy as jnp
from jax import lax
from jax.experimental import pallas as pl
from jax.experimental.pallas import tpu as pltpu

F32 = jnp.float32
BF16 = jnp.bfloat16

D_MODEL = 2048
HEADS = 16
HEAD_DIM = 128
GROUPS = 16
A_CHUNK = 128
DELTA_CHUNK = 64
CONV_W = 4
D_FF = 4 * D_MODEL
ALPHA = 2.0 ** 0.25
LN_EPS = 1e-5
RMS_EPS = 1e-6

SEG = 2048
N_MAIN = 6 * SEG
(SEG_U, SEG_VA, SEG_Q, SEG_K, SEG_V, SEG_Z) = range(6)
N_BA = 2 * HEADS
(SEG_GA, SEG_GB) = range(2)
CARRY = 16
BA_LANES = 128
VMEM_LIMIT = 48 * 1024 * 1024
FFN_VMEM_LIMIT = 56 * 1024 * 1024


def _params(*sem):
    return pltpu.CompilerParams(dimension_semantics=sem, vmem_limit_bytes=VMEM_LIMIT)


def _bdot(a, b):
    return jnp.dot(a.astype(BF16), b.astype(BF16), preferred_element_type=F32)


def _gelu(x):
    return 0.5 * x * (1.0 + lax.erf(x * (2.0 ** -0.5)))


def _sigmoid(x):
    return 1.0 / (1.0 + jnp.exp(-x))


def _silu(x):
    return x * _sigmoid(x)


def _softplus(x):
    return jnp.maximum(x, 0.0) + jnp.log1p(jnp.exp(-jnp.abs(x)))


def _layer_norm(x, g, b):
    mu = jnp.mean(x, axis=-1, keepdims=True)
    xc = x - mu
    var = jnp.mean(xc * xc, axis=-1, keepdims=True)
    return xc * lax.rsqrt(var + LN_EPS) * g + b


def _inproj_kernel(x_ref, wt_ref, o_ref, *wb_ref, gelu_panels):
    if wb_ref:
        @pl.when(pl.program_id(1) == 0)
        def _():
            wb_ref[0][...] = wt_ref[...].astype(BF16)
        w = wb_ref[0][...]
    else:
        w = wt_ref[...]
    acc = lax.dot_general(x_ref[...], w, (((1,), (1,)), ((), ())), preferred_element_type=F32)
    j = pl.program_id(0)

    @pl.when(j < gelu_panels)
    def _():
        o_ref[...] = _gelu(acc).astype(o_ref.dtype)

    @pl.when(j >= gelu_panels)
    def _():
        o_ref[...] = acc.astype(o_ref.dtype)


def _inproj(x, wt, n, out_dtype, gelu_cols, tm, tn, layer=None):
    m, k = x.shape
    cast = wt.dtype != BF16
    if layer is None:
        w_spec = pl.BlockSpec((tn, k), lambda j, i: (j, 0))
    else:
        w_spec = pl.BlockSpec((None, tn, k), lambda j, i: (layer, j, 0))
    out_shape = [jax.ShapeDtypeStruct((m, n), out_dtype)]
    out_specs = [pl.BlockSpec((tm, tn), lambda j, i: (i, j))]
    if cast:
        out_shape.append(jax.ShapeDtypeStruct((n, k), BF16))
        out_specs.append(pl.BlockSpec((tn, k), lambda j, i: (j, 0)))
    res = pl.pallas_call(
        functools.partial(_inproj_kernel, gelu_panels=gelu_cols // tn),
        out_shape=out_shape,
        grid=(n // tn, m // tm),
        in_specs=[pl.BlockSpec((tm, k), lambda j, i: (i, 0)), w_spec],
        out_specs=out_specs,
        compiler_params=_params("arbitrary", "arbitrary"),
        name="inproj",
    )(x, wt)
    return (res[0], res[1]) if cast else (res[0], wt)


def _mixer_a_kernel(u_ref, va_ref, ws_ref, bst_ref, g_ref, b_ref, ya_ref, *van_ref):
    van = _layer_norm(va_ref[...].astype(F32), g_ref[...], b_ref[...])
    if van_ref:
        van_ref[0][...] = van
    vb = van.astype(BF16)
    row = lax.broadcasted_iota(jnp.int32, (A_CHUNK, A_CHUNK), 0)
    col = lax.broadcasted_iota(jnp.int32, (A_CHUNK, A_CHUNK), 1)
    causal = row >= col
    for g in range(GROUPS):
        sl = slice(g * A_CHUNK, (g + 1) * A_CHUNK)
        w = jnp.where(causal, ws_ref[g], 0.0).astype(BF16)
        mixed = jnp.dot(w, vb[:, sl], preferred_element_type=F32) + bst_ref[:, g:g + 1]
        ya_ref[:, sl] = (u_ref[:, sl].astype(F32) * mixed).astype(ya_ref.dtype)


def _mixer_a(p, w_s, b_s_t, ln_g, ln_b, emit_van):
    m = p.shape[0]
    out_shape = [jax.ShapeDtypeStruct((m, SEG), BF16)]
    out_specs = [pl.BlockSpec((A_CHUNK, SEG), lambda i: (i, 0))]
    if emit_van:
        out_shape.append(jax.ShapeDtypeStruct((m, SEG), F32))
        out_specs.append(pl.BlockSpec((A_CHUNK, SEG), lambda i: (i, 0)))
    return pl.pallas_call(
        _mixer_a_kernel,
        out_shape=out_shape,
        grid=(m // A_CHUNK,),
        in_specs=[pl.BlockSpec((A_CHUNK, SEG), lambda i: (i, SEG_U)),
                  pl.BlockSpec((A_CHUNK, SEG), lambda i: (i, SEG_VA)),
                  pl.BlockSpec((GROUPS, A_CHUNK, A_CHUNK), lambda i: (0, 0, 0)),
                  pl.BlockSpec((A_CHUNK, GROUPS), lambda i: (0, 0)),
                  pl.BlockSpec((1, SEG), lambda i: (0, 0)),
                  pl.BlockSpec((1, SEG), lambda i: (0, 0))],
        out_specs=out_specs,
        compiler_params=_params("arbitrary"),
        name="mixer_a",
    )(p, p, w_s, b_s_t, ln_g, ln_b)


def _beta_and_g(ba, alog_row, dt_row):
    beta = _sigmoid(ba)
    g = -jnp.exp(alog_row) * _softplus(ba + dt_row)
    return beta, g


def _gated_rms_norm(o, z, w):
    o = o * lax.rsqrt(jnp.mean(o * o, axis=-1, keepdims=True) + RMS_EPS)
    return o * w * _silu(z)


def _l2norm(x):
    return x * lax.rsqrt(jnp.sum(x * x, axis=-1, keepdims=True) + RMS_EPS)


def _unit_lower_inverse(ns, masks):
    eye, blk8, merges = masks
    nd = [jnp.where(blk8, n, 0.0) for n in ns]
    n2 = [_bdot(x, x) for x in nd]
    n4 = [_bdot(x, x) for x in n2]
    p = [eye - x for x in nd]
    p = [x + _bdot(x, y) for x, y in zip(p, n2)]
    p = [x + _bdot(x, y) for x, y in zip(p, n4)]
    for cmask in merges:
        e = [_bdot(jnp.where(cmask, n, 0.0), x) for n, x in zip(ns, p)]
        p = [x - _bdot(x, y) for x, y in zip(p, e)]
    return p


def _inverse_masks():
    c = DELTA_CHUNK
    r = lax.broadcasted_iota(jnp.int32, (c, c), 0)
    q = lax.broadcasted_iota(jnp.int32, (c, c), 1)
    eye = (r == q).astype(F32)
    blk = lambda s: (r >> s) == (q >> s)
    merges = [blk(s + 1) & jnp.logical_not(blk(s)) for s in (3, 4, 5)]
    return (eye, blk(3), merges), r >= q, r > q


def _delta_chunk_kernel(q_ref, k_ref, v_ref, z_ref, ba_ref, wq_ref, wk_ref, wv_ref, alog_ref, dt_ref,
                        won_ref, y_ref, s_out_ref, s_ref, carry_ref):
    c = pl.program_id(1)
    nc = pl.num_programs(1)
    C = DELTA_CHUNK

    @pl.when(c == 0)
    def _():
        s_ref[...] = jnp.zeros_like(s_ref)
        carry_ref[...] = jnp.zeros_like(carry_ref)

    sr = lax.broadcasted_iota(jnp.int32, ((CONV_W - 1) * C, CARRY + C), 0)
    sc = lax.broadcasted_iota(jnp.int32, ((CONV_W - 1) * C, CARRY + C), 1)
    tap = sr >> (C.bit_length() - 1)
    shift = jnp.where(sc - (sr & (C - 1)) - tap == CARRY - (CONV_W - 1), 1.0, 0.0).astype(BF16)

    def conv_silu(x_ref, w_ref, seg):
        cur = x_ref[...]
        ext = jnp.concatenate([carry_ref[seg], cur], axis=0)
        sh = jnp.dot(shift, ext, preferred_element_type=F32)
        acc = cur.astype(F32) * w_ref[CONV_W - 1:CONV_W, :]
        for j in range(CONV_W - 1):
            acc = acc + sh[j * C:(j + 1) * C, :] * w_ref[j:j + 1, :]
        carry_ref[seg] = cur[C - CARRY:C, :]
        return _silu(acc)

    qa = conv_silu(q_ref, wq_ref, 0)
    ka = conv_silu(k_ref, wk_ref, 1)
    va = conv_silu(v_ref, wv_ref, 2)

    beta, g = _beta_and_g(ba_ref[...], alog_ref[...], dt_ref[...])
    row = lax.broadcasted_iota(jnp.int32, (C, BA_LANES), 0)
    gc = g
    for s in (1, 2, 4, 8, 16, 32):
        gc = gc + jnp.where(row >= s, pltpu.roll(gc, s, axis=0), 0.0)
    gc_t = gc.T
    g_last = gc[C - 1:C, :]
    e_gc = jnp.exp(gc)
    e_rest = jnp.exp(g_last - gc)
    e_last = jnp.exp(g_last)

    inv_masks, incl, strict = _inverse_masks()
    w_on = won_ref[...]

    hs = range(HEADS)
    sl = [slice(h * HEAD_DIM, (h + 1) * HEAD_DIM) for h in hs]
    col = lambda x, h: x[:, HEADS + h:HEADS + h + 1]
    q = [_l2norm(qa[:, sl[h]]) * (HEAD_DIM ** -0.5) for h in hs]
    k = [_l2norm(ka[:, sl[h]]) for h in hs]
    kb = [k[h] * beta[:, h:h + 1] for h in hs]
    vb = [va[:, sl[h]] * beta[:, h:h + 1] for h in hs]
    decay = [jnp.exp(jnp.where(incl, col(gc, h) - gc_t[HEADS + h:HEADS + h + 1, :], -jnp.inf)) for h in hs]
    kk = [lax.dot_general(jnp.concatenate([kb[h], q[h]], axis=0).astype(BF16), k[h].astype(BF16),
                          (((1,), (1,)), ((), ())), preferred_element_type=F32) for h in hs]
    n = [jnp.where(strict, kk[h][:C] * decay[h], 0.0) for h in hs]
    attn = [kk[h][C:] * decay[h] for h in hs]
    t = _unit_lower_inverse(n, inv_masks)
    tr = [_bdot(t[h], jnp.concatenate([vb[h], kb[h] * col(e_gc, h)], axis=1)) for h in hs]
    m1 = [_bdot(jnp.concatenate([tr[h][:, HEAD_DIM:], q[h] * col(e_gc, h)], axis=0), s_ref[h]) for h in hs]
    v_new = [tr[h][:, :HEAD_DIM] - m1[h][:C] for h in hs]
    o = [m1[h][C:] + _bdot(attn[h], v_new[h]) for h in hs]
    kv = [lax.dot_general((k[h] * col(e_rest, h)).astype(BF16), v_new[h].astype(BF16),
                          (((0,), (0,)), ((), ())), preferred_element_type=F32) for h in hs]
    for h in hs:
        s_ref[h] = s_ref[h] * col(e_last, h) + kv[h]
        y_ref[:, sl[h]] = _gated_rms_norm(o[h], z_ref[:, sl[h]].astype(F32), w_on).astype(y_ref.dtype)

    @pl.when(c == nc - 1)
    def _():
        s_out_ref[0] = s_ref[...]


def _delta_chunks(p, ba, w_conv, alog_row, dt_row, w_onorm, batch, seq):
    C = DELTA_CHUNK
    nc = seq // C
    rows = lambda seg: pl.BlockSpec((C, SEG), lambda b, c: (b * nc + c, seg))
    wspec = lambda seg: pl.BlockSpec((CONV_W, SEG), lambda b, c: (0, seg))
    small = pl.BlockSpec((1, BA_LANES), lambda b, c: (0, 0))
    return pl.pallas_call(
        _delta_chunk_kernel,
        out_shape=[jax.ShapeDtypeStruct((batch * seq, SEG), BF16),
                   jax.ShapeDtypeStruct((batch, HEADS, HEAD_DIM, HEAD_DIM), F32)],
        grid=(batch, nc),
        in_specs=[rows(SEG_Q), rows(SEG_K), rows(SEG_V), rows(SEG_Z),
                  pl.BlockSpec((C, BA_LANES), lambda b, c: (b * nc + c, 0)),
                  wspec(0), wspec(1), wspec(2), small, small, small],
        out_specs=[pl.BlockSpec((C, SEG), lambda b, c: (b * nc + c, 0)),
                   pl.BlockSpec((1, HEADS, HEAD_DIM, HEAD_DIM), lambda b, c: (b, 0, 0, 0))],
        scratch_shapes=[pltpu.VMEM((HEADS, HEAD_DIM, HEAD_DIM), F32),
                        pltpu.VMEM((3, CARRY, SEG), BF16)],
        compiler_params=_params("arbitrary", "arbitrary"),
        name="delta_chunks",
    )(p, p, p, p, ba, w_conv, w_conv, w_conv, alog_row, dt_row, w_onorm)


ROWS = 8


def _delta_step_kernel(q_ref, k_ref, v_ref, z_ref, ba_ref, cs_ref, wq_ref, wk_ref, wv_ref, alog_ref,
                       dt_ref, won_ref, s_ref, y_ref, s_out_ref):
    r = pl.program_id(0) % ROWS
    rsel = lax.broadcasted_iota(jnp.int32, (ROWS, 1), 0) == r

    def pick(x):
        return jnp.sum(jnp.where(rsel, x.astype(F32), 0.0), axis=0, keepdims=True)

    def conv_silu(x_ref, w_ref, seg):
        acc = pick(x_ref[...]) * w_ref[3:4, :]
        for j in range(CONV_W - 1):
            acc = acc + cs_ref[0, j:j + 1, seg * SEG:(seg + 1) * SEG] * w_ref[j:j + 1, :]
        return _silu(acc)

    qa = conv_silu(q_ref, wq_ref, 0)
    ka = conv_silu(k_ref, wk_ref, 1)
    va = conv_silu(v_ref, wv_ref, 2)
    z = pick(z_ref[...])
    beta, g = _beta_and_g(pick(ba_ref[...]), alog_ref[...], dt_ref[...])
    e_g = jnp.exp(g)
    w_on = won_ref[...]
    pad = jnp.zeros((ROWS - 2, HEAD_DIM), F32)

    hs = range(HEADS)
    sl = [slice(h * HEAD_DIM, (h + 1) * HEAD_DIM) for h in hs]
    q = [_l2norm(qa[:, sl[h]]) * (HEAD_DIM ** -0.5) for h in hs]
    k = [_l2norm(ka[:, sl[h]]) for h in hs]
    eg = [e_g[:, HEADS + h:HEADS + h + 1] for h in hs]
    ks_qs = [_bdot(jnp.concatenate([k[h], q[h], pad], axis=0), s_ref[0, h]) for h in hs]
    v_new = [beta[:, h:h + 1] * (va[:, sl[h]] - eg[h] * ks_qs[h][0:1]) for h in hs]
    o = [eg[h] * ks_qs[h][1:2] + jnp.sum(q[h] * k[h], axis=-1, keepdims=True) * v_new[h] for h in hs]
    zero_row = jnp.zeros((1, HEAD_DIM), F32)
    kv = [lax.dot_general(jnp.concatenate([k[h], pad, zero_row], axis=0).astype(BF16),
                          jnp.broadcast_to(v_new[h], (ROWS, HEAD_DIM)).astype(BF16),
                          (((0,), (0,)), ((), ())), preferred_element_type=F32) for h in hs]
    for h in hs:
        s_out_ref[0, h] = s_ref[0, h] * eg[h] + kv[h]
    outs = [_gated_rms_norm(o[h], z[:, sl[h]], w_on) for h in hs]
    y_row = jnp.concatenate(outs, axis=1)

    @pl.when(r == 0)
    def _():
        y_ref[...] = jnp.broadcast_to(y_row, y_ref.shape).astype(y_ref.dtype)

    @pl.when(r != 0)
    def _():
        y_ref[...] = jnp.where(rsel, y_row, y_ref[...].astype(F32)).astype(y_ref.dtype)


def _delta_step(p, ba, conv_state, ssm_state, w_conv, alog_row, dt_row, w_onorm):
    n = p.shape[0]
    rows = lambda seg: pl.BlockSpec((ROWS, SEG), lambda b: (b // ROWS, seg))
    wspec = lambda seg: pl.BlockSpec((CONV_W, SEG), lambda b: (0, seg))
    small = pl.BlockSpec((1, BA_LANES), lambda b: (0, 0))
    sspec = pl.BlockSpec((1, HEADS, HEAD_DIM, HEAD_DIM), lambda b: (b, 0, 0, 0))
    return pl.pallas_call(
        _delta_step_kernel,
        out_shape=[jax.ShapeDtypeStruct((n, SEG), BF16),
                   jax.ShapeDtypeStruct(ssm_state.shape, F32)],
        grid=(n,),
        in_specs=[rows(SEG_Q), rows(SEG_K), rows(SEG_V), rows(SEG_Z),
                  pl.BlockSpec((ROWS, BA_LANES), lambda b: (b // ROWS, 0)),
                  pl.BlockSpec((1, CONV_W - 1, 3 * SEG), lambda b: (b, 0, 0)),
                  wspec(0), wspec(1), wspec(2), small, small, small, sspec],
        out_specs=[pl.BlockSpec((ROWS, SEG), lambda b: (b // ROWS, 0)), sspec],
        compiler_params=_params("arbitrary"),
        name="delta_step",
    )(p, p, p, p, ba, conv_state, w_conv, w_conv, w_conv, alog_row, dt_row, w_onorm, ssm_state)


def _merge_kernel(ya_ref, yb_ref, ga_ref, gb_ref, wa_ref, wb_ref, m_ref):
    pa = jnp.dot(ya_ref[...], wa_ref[...], preferred_element_type=F32)
    pb = jnp.dot(yb_ref[...], wb_ref[...], preferred_element_type=F32)
    m = _sigmoid(ga_ref[...].astype(F32)) * pa + _sigmoid(gb_ref[...].astype(F32)) * pb
    m_ref[...] = m.astype(m_ref.dtype)


def _merge(ya, yb, gates, wa, wb, tm, tn):
    m = ya.shape[0]
    per_seg = SEG // tn
    return pl.pallas_call(
        _merge_kernel,
        out_shape=jax.ShapeDtypeStruct((m, SEG), BF16),
        grid=(per_seg, m // tm),
        in_specs=[pl.BlockSpec((tm, SEG), lambda j, i: (i, 0)),
                  pl.BlockSpec((tm, SEG), lambda j, i: (i, 0)),
                  pl.BlockSpec((tm, tn), lambda j, i: (i, SEG_GA * per_seg + j)),
                  pl.BlockSpec((tm, tn), lambda j, i: (i, SEG_GB * per_seg + j)),
                  pl.BlockSpec((SEG, tn), lambda j, i: (0, j)),
                  pl.BlockSpec((SEG, tn), lambda j, i: (0, j))],
        out_specs=pl.BlockSpec((tm, tn), lambda j, i: (i, j)),
        compiler_params=_params("arbitrary", "arbitrary"),
        name="merge",
    )(ya, yb, gates, gates, wa, wb)


def _out_ln_kernel(x_ref, m_ref, w_ref, g_ref, b_ref, o_ref):
    y = ALPHA * x_ref[...] + jnp.dot(m_ref[...], w_ref[...], preferred_element_type=F32)
    o_ref[...] = _layer_norm(y, g_ref[...], b_ref[...])


def _out_ln(x, m, w, g, b, tm):
    rows = x.shape[0]
    vec = pl.BlockSpec((1, D_MODEL), lambda i: (0, 0))
    return pl.pallas_call(
        _out_ln_kernel,
        out_shape=jax.ShapeDtypeStruct((rows, D_MODEL), F32),
        grid=(rows // tm,),
        in_specs=[pl.BlockSpec((tm, D_MODEL), lambda i: (i, 0)),
                  pl.BlockSpec((tm, SEG), lambda i: (i, 0)),
                  pl.BlockSpec((SEG, D_MODEL), lambda i: (0, 0)),
                  vec, vec],
        out_specs=pl.BlockSpec((tm, D_MODEL), lambda i: (i, 0)),
        compiler_params=_params("arbitrary"),
        name="out_ln",
    )(x, m, w, g, b)


def _ffn_kernel(x_ref, wu_ref, wd_ref, g_ref, b_ref, o_ref, xb_ref):
    f = pl.program_id(1)

    @pl.when(f == 0)
    def _():
        xb_ref[...] = x_ref[...].astype(BF16)

    h = jnp.dot(xb_ref[...], wu_ref[...], preferred_element_type=F32)
    h = jnp.square(jnp.maximum(h, 0.0)).astype(BF16)
    part = jnp.dot(h, wd_ref[...], preferred_element_type=F32)

    @pl.when(f == 0)
    def _():
        o_ref[...] = part

    @pl.when(f != 0)
    def _():
        o_ref[...] += part

    @pl.when(f == pl.num_programs(1) - 1)
    def _():
        o_ref[...] = _layer_norm(ALPHA * x_ref[...] + o_ref[...], g_ref[...], b_ref[...])


def _ffn(x, wu, wd, g, b, tm, tf):
    rows = x.shape[0]
    vec = pl.BlockSpec((1, D_MODEL), lambda i, f: (0, 0))
    return pl.pallas_call(
        _ffn_kernel,
        out_shape=jax.ShapeDtypeStruct((rows, D_MODEL), F32),
        grid=(rows // tm, D_FF // tf),
        in_specs=[pl.BlockSpec((tm, D_MODEL), lambda i, f: (i, 0), pipeline_mode=pl.Buffered(1)),
                  pl.BlockSpec((D_MODEL, tf), lambda i, f: (0, f)),
                  pl.BlockSpec((tf, D_MODEL), lambda i, f: (f, 0)),
                  vec, vec],
        out_specs=pl.BlockSpec((tm, D_MODEL), lambda i, f: (i, 0)),
        scratch_shapes=[pltpu.VMEM((tm, D_MODEL), BF16)],
        compiler_params=pltpu.CompilerParams(dimension_semantics=("arbitrary", "arbitrary"),
                                             vmem_limit_bytes=FFN_VMEM_LIMIT),
        name="ffn",
    )(x, wu, wd, g, b)


def _row_tile(m):
    return min(m, 1024)


def _lane_row(vals, offset):
    return jnp.zeros((1, BA_LANES), F32).at[0, offset:offset + vals.shape[0]].set(vals)


def _layer(x, conv_state, ssm_state, wts, batch, seq):
    (w_main, w_gates, w_ba, w_s, b_s, ln_v_g, ln_v_b, w_conv, a_log, dt_bias, w_onorm, w_pa, w_pb, w_o,
     ln1_g, ln1_b, w_up, w_down, ln2_g, ln2_b) = wts
    m = x.shape[0]
    tm = _row_tile(m)
    xb = x.astype(BF16)
    if isinstance(w_main, tuple):
        p, w_main = _inproj(xb, w_main[0], N_MAIN, BF16, 2 * SEG, tm, 1024, layer=w_main[1])
    else:
        p, w_main = _inproj(xb, w_main, N_MAIN, BF16, 2 * SEG, tm, 1024)
    gates, w_gates = _inproj(xb, w_gates, 2 * SEG, BF16, 0, tm, 1024)
    ba, w_ba = _inproj(xb, w_ba, BA_LANES, F32, 0, tm, BA_LANES)

    row = lambda v: v.reshape(1, -1)
    alog_row = _lane_row(a_log, HEADS)
    dt_row = _lane_row(dt_bias, HEADS)
    if conv_state is None:
        (ya,) = _mixer_a(p, w_s, b_s.T, row(ln_v_g), row(ln_v_b), emit_van=False)
        van = None
        yb, ssm_new = _delta_chunks(p, ba, w_conv, alog_row, dt_row, row(w_onorm), batch, seq)
        tail = p.reshape(batch, seq, N_MAIN)[:, seq - (CONV_W - 1):, SEG_Q * SEG:(SEG_V + 1) * SEG]
        conv_new = tail.astype(F32)
    else:
        w_s0 = w_s[:, 0, 0][:, None, None] * jnp.eye(A_CHUNK, dtype=F32)[None]
        b_s0 = jnp.broadcast_to(b_s[:, 0][None, :], (A_CHUNK, GROUPS))
        ya, van = _mixer_a(p, w_s0, b_s0, row(ln_v_g), row(ln_v_b), emit_van=True)
        yb, ssm_new = _delta_step(p, ba, conv_state, ssm_state, w_conv, alog_row, dt_row, row(w_onorm))
        qkv = p[:, SEG_Q * SEG:(SEG_V + 1) * SEG].astype(F32)
        conv_new = jnp.concatenate([conv_state[:, 1:, :], qkv[:, None, :]], axis=1)

    mg = _merge(ya, yb, gates, w_pa, w_pb, tm, 512)
    x1 = _out_ln(x, mg, w_o, row(ln1_g), row(ln1_b), min(m, 512))
    y = _ffn(x1, w_up, w_down, row(ln2_g), row(ln2_b), tm, 512)
    wts = (w_main, w_gates, w_ba) + tuple(wts[3:])
    return (y, van, conv_new, ssm_new), wts


def kernel(x_prompt, x_sample, state_conv, state_ssm, w_in, w_s, b_s, ln_v_g, ln_v_b, w_conv, a_log,
           dt_bias, w_onorm, w_proj_a, w_proj_b, w_o, ln1_g, ln1_b, w_up, w_down, ln2_g, ln2_b):
    depth = w_in.shape[0]
    bp, tp, _ = x_prompt.shape
    bs, ts, _ = x_sample.shape
    assert ts == 1, "the sample group advances one token per sequence"
    yp = x_prompt.reshape(bp * tp, D_MODEL)
    ys = x_sample.reshape(bs, D_MODEL)
    conv_p, ssm_p, vrows_s, conv_s, ssm_s = [], [], [], [], []
    for l in range(depth):
        w_in_t = jnp.swapaxes(w_in, 1, 2)
        w_gates = w_in_t[l, N_MAIN + N_BA:, :]
        w_ba = jnp.pad(w_in_t[l, N_MAIN:N_MAIN + N_BA, :], ((0, BA_LANES - N_BA), (0, 0)))
        wts = ((w_in_t, l), w_gates, w_ba, w_s[l], b_s[l], ln_v_g[l], ln_v_b[l], w_conv[l], a_log[l], dt_bias[l],
               w_onorm[l], w_proj_a[l].astype(BF16), w_proj_b[l].astype(BF16), w_o[l].astype(BF16),
               ln1_g[l], ln1_b[l], w_up[l].astype(BF16), w_down[l].astype(BF16), ln2_g[l], ln2_b[l])
        (yp, _, cp, sp), wts = _layer(yp, None, None, wts, bp, tp)
        (ys, vs, cs, ss), _ = _layer(ys, state_conv[l], state_ssm[l], wts, bs, 1)
        conv_p.append(cp)
        ssm_p.append(sp)
        vrows_s.append(vs.reshape(bs, 1, SEG))
        conv_s.append(cs)
        ssm_s.append(ss)
    return (yp.reshape(bp, tp, D_MODEL), ys.reshape(bs, 1, D_MODEL), jnp.stack(conv_p), jnp.stack(ssm_p),
            jnp.stack(vrows_s), jnp.stack(conv_s), jnp.stack(ssm_s))
```

```python
import functools

import jax
import jax.numpy as jnp
from jax import lax
from jax.experimental import pallas as pl
from jax.experimental.pallas import tpu as pltpu

F32 = jnp.float32
BF16 = jnp.bfloat16

D_MODEL = 2048
HEADS = 16
HEAD_DIM = 128
GROUPS = 16
A_CHUNK = 128
DELTA_CHUNK = 64
CONV_W = 4
D_FF = 4 * D_MODEL
ALPHA = 2.0 ** 0.25
LN_EPS = 1e-5
RMS_EPS = 1e-6

SEG = 2048
N_MAIN = 6 * SEG
(SEG_U, SEG_VA, SEG_Q, SEG_K, SEG_V, SEG_Z) = range(6)
N_BA = 2 * HEADS
(SEG_GA, SEG_GB) = range(2)
CARRY = 16
BA_LANES = 128
VMEM_LIMIT = 48 * 1024 * 1024
FFN_VMEM_LIMIT = 56 * 1024 * 1024


def _params(*sem):
    return pltpu.CompilerParams(dimension_semantics=sem, vmem_limit_bytes=VMEM_LIMIT)


def _bdot(a, b):
    return jnp.dot(a.astype(BF16), b.astype(BF16), preferred_element_type=F32)


def _gelu(x):
    return 0.5 * x * (1.0 + lax.erf(x * (2.0 ** -0.5)))


def _sigmoid(x):
    return 1.0 / (1.0 + jnp.exp(-x))


def _silu(x):
    return x * _sigmoid(x)


def _softplus(x):
    return jnp.maximum(x, 0.0) + jnp.log1p(jnp.exp(-jnp.abs(x)))


def _layer_norm(x, g, b):
    mu = jnp.mean(x, axis=-1, keepdims=True)
    xc = x - mu
    var = jnp.mean(xc * xc, axis=-1, keepdims=True)
    return xc * lax.rsqrt(var + LN_EPS) * g + b


def _inproj_kernel(x_ref, wt_ref, o_ref, *wb_ref, gelu_panels):
    if wb_ref:
        @pl.when(pl.program_id(1) == 0)
        def _():
            wb_ref[0][...] = wt_ref[...].astype(BF16)
        w = wb_ref[0][...]
    else:
        w = wt_ref[...]
    acc = lax.dot_general(x_ref[...], w, (((1,), (1,)), ((), ())), preferred_element_type=F32)
    j = pl.program_id(0)

    @pl.when(j < gelu_panels)
    def _():
        o_ref[...] = _gelu(acc).astype(o_ref.dtype)

    @pl.when(j >= gelu_panels)
    def _():
        o_ref[...] = acc.astype(o_ref.dtype)


def _inproj(x, wt, n, out_dtype, gelu_cols, tm, tn, layer=None):
    m, k = x.shape
    cast = wt.dtype != BF16
    if layer is None:
        w_spec = pl.BlockSpec((tn, k), lambda j, i: (j, 0))
    else:
        w_spec = pl.BlockSpec((None, tn, k), lambda j, i: (layer, j, 0))
    out_shape = [jax.ShapeDtypeStruct((m, n), out_dtype)]
    out_specs = [pl.BlockSpec((tm, tn), lambda j, i: (i, j))]
    if cast:
        out_shape.append(jax.ShapeDtypeStruct((n, k), BF16))
        out_specs.append(pl.BlockSpec((tn, k), lambda j, i: (j, 0)))
    res = pl.pallas_call(
        functools.partial(_inproj_kernel, gelu_panels=gelu_cols // tn),
        out_shape=out_shape,
        grid=(n // tn, m // tm),
        in_specs=[pl.BlockSpec((tm, k), lambda j, i: (i, 0)), w_spec],
        out_specs=out_specs,
        compiler_params=_params("arbitrary", "arbitrary"),
        name="inproj",
    )(x, wt)
    return (res[0], res[1]) if cast else (res[0], wt)


def _mixer_a_kernel(u_ref, va_ref, ws_ref, bst_ref, g_ref, b_ref, ya_ref, *van_ref):
    van = _layer_norm(va_ref[...].astype(F32), g_ref[...], b_ref[...])
    if van_ref:
        van_ref[0][...] = van
    vb = van.astype(BF16)
    row = lax.broadcasted_iota(jnp.int32, (A_CHUNK, A_CHUNK), 0)
    col = lax.broadcasted_iota(jnp.int32, (A_CHUNK, A_CHUNK), 1)
    causal = row >= col
    for g in range(GROUPS):
        sl = slice(g * A_CHUNK, (g + 1) * A_CHUNK)
        w = jnp.where(causal, ws_ref[g], 0.0).astype(BF16)
        mixed = jnp.dot(w, vb[:, sl], preferred_element_type=F32) + bst_ref[:, g:g + 1]
        ya_ref[:, sl] = (u_ref[:, sl].astype(F32) * mixed).astype(ya_ref.dtype)


def _mixer_a(p, w_s, b_s_t, ln_g, ln_b, emit_van):
    m = p.shape[0]
    out_shape = [jax.ShapeDtypeStruct((m, SEG), BF16)]
    out_specs = [pl.BlockSpec((A_CHUNK, SEG), lambda i: (i, 0))]
    if emit_van:
        out_shape.append(jax.ShapeDtypeStruct((m, SEG), F32))
        out_specs.append(pl.BlockSpec((A_CHUNK, SEG), lambda i: (i, 0)))
    return pl.pallas_call(
        _mixer_a_kernel,
        out_shape=out_shape,
        grid=(m // A_CHUNK,),
        in_specs=[pl.BlockSpec((A_CHUNK, SEG), lambda i: (i, SEG_U)),
                  pl.BlockSpec((A_CHUNK, SEG), lambda i: (i, SEG_VA)),
                  pl.BlockSpec((GROUPS, A_CHUNK, A_CHUNK), lambda i: (0, 0, 0)),
                  pl.BlockSpec((A_CHUNK, GROUPS), lambda i: (0, 0)),
                  pl.BlockSpec((1, SEG), lambda i: (0, 0)),
                  pl.BlockSpec((1, SEG), lambda i: (0, 0))],
        out_specs=out_specs,
        compiler_params=_params("arbitrary"),
        name="mixer_a",
    )(p, p, w_s, b_s_t, ln_g, ln_b)


def _beta_and_g(ba, alog_row, dt_row):
    beta = _sigmoid(ba)
    g = -jnp.exp(alog_row) * _softplus(ba + dt_row)
    return beta, g


def _gated_rms_norm(o, z, w):
    o = o * lax.rsqrt(jnp.mean(o * o, axis=-1, keepdims=True) + RMS_EPS)
    return o * w * _silu(z)


def _l2norm(x):
    return x * lax.rsqrt(jnp.sum(x * x, axis=-1, keepdims=True) + RMS_EPS)


def _unit_lower_inverse(ns, masks):
    eye, blk8, merges = masks
    nd = [jnp.where(blk8, n, 0.0) for n in ns]
    n2 = [_bdot(x, x) for x in nd]
    n4 = [_bdot(x, x) for x in n2]
    p = [eye - x for x in nd]
    p = [x + _bdot(x, y) for x, y in zip(p, n2)]
    p = [x + _bdot(x, y) for x, y in zip(p, n4)]
    for cmask in merges:
        e = [_bdot(jnp.where(cmask, n, 0.0), x) for n, x in zip(ns, p)]
        p = [x - _bdot(x, y) for x, y in zip(p, e)]
    return p


def _inverse_masks():
    c = DELTA_CHUNK
    r = lax.broadcasted_iota(jnp.int32, (c, c), 0)
    q = lax.broadcasted_iota(jnp.int32, (c, c), 1)
    eye = (r == q).astype(F32)
    blk = lambda s: (r >> s) == (q >> s)
    merges = [blk(s + 1) & jnp.logical_not(blk(s)) for s in (3, 4, 5)]
    return (eye, blk(3), merges), r >= q, r > q


def _delta_chunk_kernel(q_ref, k_ref, v_ref, z_ref, ba_ref, wq_ref, wk_ref, wv_ref, alog_ref, dt_ref,
                        won_ref, y_ref, s_out_ref, s_ref, carry_ref):
    c = pl.program_id(1)
    nc = pl.num_programs(1)
    C = DELTA_CHUNK

    @pl.when(c == 0)
    def _():
        s_ref[...] = jnp.zeros_like(s_ref)
        carry_ref[...] = jnp.zeros_like(carry_ref)

    sr = lax.broadcasted_iota(jnp.int32, ((CONV_W - 1) * C, CARRY + C), 0)
    sc = lax.broadcasted_iota(jnp.int32, ((CONV_W - 1) * C, CARRY + C), 1)
    tap = sr >> (C.bit_length() - 1)
    shift = jnp.where(sc - (sr & (C - 1)) - tap == CARRY - (CONV_W - 1), 1.0, 0.0).astype(BF16)

    def conv_silu(x_ref, w_ref, seg):
        cur = x_ref[...]
        ext = jnp.concatenate([carry_ref[seg], cur], axis=0)
        sh = jnp.dot(shift, ext, preferred_element_type=F32)
        acc = cur.astype(F32) * w_ref[CONV_W - 1:CONV_W, :]
        for j in range(CONV_W - 1):
            acc = acc + sh[j * C:(j + 1) * C, :] * w_ref[j:j + 1, :]
        carry_ref[seg] = cur[C - CARRY:C, :]
        return _silu(acc)

    qa = conv_silu(q_ref, wq_ref, 0)
    ka = conv_silu(k_ref, wk_ref, 1)
    va = conv_silu(v_ref, wv_ref, 2)

    beta, g = _beta_and_g(ba_ref[...], alog_ref[...], dt_ref[...])
    row = lax.broadcasted_iota(jnp.int32, (C, BA_LANES), 0)
    gc = g
    for s in (1, 2, 4, 8, 16, 32):
        gc = gc + jnp.where(row >= s, pltpu.roll(gc, s, axis=0), 0.0)
    gc_t = gc.T
    g_last = gc[C - 1:C, :]
    e_gc = jnp.exp(gc)
    e_rest = jnp.exp(g_last - gc)
    e_last = jnp.exp(g_last)

    inv_masks, incl, strict = _inverse_masks()
    w_on = won_ref[...]

    hs = range(HEADS)
    sl = [slice(h * HEAD_DIM, (h + 1) * HEAD_DIM) for h in hs]
    col = lambda x, h: x[:, HEADS + h:HEADS + h + 1]
    q = [_l2norm(qa[:, sl[h]]) * (HEAD_DIM ** -0.5) for h in hs]
    k = [_l2norm(ka[:, sl[h]]) for h in hs]
    kb = [k[h] * beta[:, h:h + 1] for h in hs]
    vb = [va[:, sl[h]] * beta[:, h:h + 1] for h in hs]
    decay = [jnp.exp(jnp.where(incl, col(gc, h) - gc_t[HEADS + h:HEADS + h + 1, :], -jnp.inf)) for h in hs]
    kk = [lax.dot_general(jnp.concatenate([kb[h], q[h]], axis=0).astype(BF16), k[h].astype(BF16),
                          (((1,), (1,)), ((), ())), preferred_element_type=F32) for h in hs]
    n = [jnp.where(strict, kk[h][:C] * decay[h], 0.0) for h in hs]
    attn = [kk[h][C:] * decay[h] for h in hs]
    t = _unit_lower_inverse(n, inv_masks)
    tr = [_bdot(t[h], jnp.concatenate([vb[h], kb[h] * col(e_gc, h)], axis=1)) for h in hs]
    m1 = [_bdot(jnp.concatenate([tr[h][:, HEAD_DIM:], q[h] * col(e_gc, h)], axis=0), s_ref[h]) for h in hs]
    v_new = [tr[h][:, :HEAD_DIM] - m1[h][:C] for h in hs]
    o = [m1[h][C:] + _bdot(attn[h], v_new[h]) for h in hs]
    kv = [lax.dot_general((k[h] * col(e_rest, h)).astype(BF16), v_new[h].astype(BF16),
                          (((0,), (0,)), ((), ())), preferred_element_type=F32) for h in hs]
    for h in hs:
        s_ref[h] = s_ref[h] * col(e_last, h) + kv[h]
        y_ref[:, sl[h]] = _gated_rms_norm(o[h], z_ref[:, sl[h]].astype(F32), w_on).astype(y_ref.dtype)

    @pl.when(c == nc - 1)
    def _():
        s_out_ref[0] = s_ref[...]


def _delta_chunks(p, ba, w_conv, alog_row, dt_row, w_onorm, batch, seq):
    C = DELTA_CHUNK
    nc = seq // C
    rows = lambda seg: pl.BlockSpec((C, SEG), lambda b, c: (b * nc + c, seg))
    wspec = lambda seg: pl.BlockSpec((CONV_W, SEG), lambda b, c: (0, seg))
    small = pl.BlockSpec((1, BA_LANES), lambda b, c: (0, 0))
    return pl.pallas_call(
        _delta_chunk_kernel,
        out_shape=[jax.ShapeDtypeStruct((batch * seq, SEG), BF16),
                   jax.ShapeDtypeStruct((batch, HEADS, HEAD_DIM, HEAD_DIM), F32)],
        grid=(batch, nc),
        in_specs=[rows(SEG_Q), rows(SEG_K), rows(SEG_V), rows(SEG_Z),
                  pl.BlockSpec((C, BA_LANES), lambda b, c: (b * nc + c, 0)),
                  wspec(0), wspec(1), wspec(2), small, small, small],
        out_specs=[pl.BlockSpec((C, SEG), lambda b, c: (b * nc + c, 0)),
                   pl.BlockSpec((1, HEADS, HEAD_DIM, HEAD_DIM), lambda b, c: (b, 0, 0, 0))],
        scratch_shapes=[pltpu.VMEM((HEADS, HEAD_DIM, HEAD_DIM), F32),
                        pltpu.VMEM((3, CARRY, SEG), BF16)],
        compiler_params=_params("arbitrary", "arbitrary"),
        name="delta_chunks",
    )(p, p, p, p, ba, w_conv, w_conv, w_conv, alog_row, dt_row, w_onorm)


ROWS = 8
SEQS = 4


def _delta_step_kernel(q_ref, k_ref, v_ref, z_ref, ba_ref, cs_ref, wq_ref, wk_ref, wv_ref, alog_ref,
                       dt_ref, won_ref, s_ref, y_ref, s_out_ref):
    part = pl.program_id(0) % (ROWS // SEQS)
    row_id = lax.broadcasted_iota(jnp.int32, (ROWS, 1), 0)
    w_on = won_ref[...]
    pad = jnp.zeros((ROWS - 2, HEAD_DIM), F32)
    zero_row = jnp.zeros((1, HEAD_DIM), F32)
    hs = range(HEADS)
    sl = [slice(h * HEAD_DIM, (h + 1) * HEAD_DIM) for h in hs]

    def one_sequence(i):
        rsel = row_id == part * SEQS + i

        def pick(x):
            return jnp.sum(jnp.where(rsel, x.astype(F32), 0.0), axis=0, keepdims=True)

        def conv_silu(x_ref, w_ref, seg):
            acc = pick(x_ref[...]) * w_ref[CONV_W - 1:CONV_W, :]
            for j in range(CONV_W - 1):
                acc = acc + cs_ref[i, j:j + 1, seg * SEG:(seg + 1) * SEG] * w_ref[j:j + 1, :]
            return _silu(acc)

        qa = conv_silu(q_ref, wq_ref, 0)
        ka = conv_silu(k_ref, wk_ref, 1)
        va = conv_silu(v_ref, wv_ref, 2)
        z = pick(z_ref[...])
        beta, g = _beta_and_g(pick(ba_ref[...]), alog_ref[...], dt_ref[...])
        e_g = jnp.exp(g)
        q = [_l2norm(qa[:, sl[h]]) * (HEAD_DIM ** -0.5) for h in hs]
        k = [_l2norm(ka[:, sl[h]]) for h in hs]
        eg = [e_g[:, HEADS + h:HEADS + h + 1] for h in hs]
        ks_qs = [_bdot(jnp.concatenate([k[h], q[h], pad], axis=0), s_ref[i, h]) for h in hs]
        v_new = [beta[:, h:h + 1] * (va[:, sl[h]] - eg[h] * ks_qs[h][0:1]) for h in hs]
        o = [eg[h] * ks_qs[h][1:2] + jnp.sum(q[h] * k[h], axis=-1, keepdims=True) * v_new[h] for h in hs]
        kv = [lax.dot_general(jnp.concatenate([k[h], pad, zero_row], axis=0).astype(BF16),
                              jnp.broadcast_to(v_new[h], (ROWS, HEAD_DIM)).astype(BF16),
                              (((0,), (0,)), ((), ())), preferred_element_type=F32) for h in hs]
        for h in hs:
            s_out_ref[i, h] = s_ref[i, h] * eg[h] + kv[h]
        outs = [_gated_rms_norm(o[h], z[:, sl[h]], w_on) for h in hs]
        return rsel, jnp.concatenate(outs, axis=1)

    done = [one_sequence(i) for i in range(SEQS)]

    def merged(base):
        for rsel, y_row in done:
            base = jnp.where(rsel, y_row, base)
        return base.astype(y_ref.dtype)

    @pl.when(part == 0)
    def _():
        y_ref[...] = merged(jnp.broadcast_to(done[0][1], y_ref.shape))

    @pl.when(part != 0)
    def _():
        y_ref[...] = merged(y_ref[...].astype(F32))


def _delta_step(p, ba, conv_state, ssm_state, w_conv, alog_row, dt_row, w_onorm):
    n = p.shape[0]
    rows = lambda seg: pl.BlockSpec((ROWS, SEG), lambda b: (b * SEQS // ROWS, seg))
    wspec = lambda seg: pl.BlockSpec((CONV_W, SEG), lambda b: (0, seg))
    small = pl.BlockSpec((1, BA_LANES), lambda b: (0, 0))
    sspec = pl.BlockSpec((SEQS, HEADS, HEAD_DIM, HEAD_DIM), lambda b: (b, 0, 0, 0))
    return pl.pallas_call(
        _delta_step_kernel,
        out_shape=[jax.ShapeDtypeStruct((n, SEG), BF16),
                   jax.ShapeDtypeStruct(ssm_state.shape, F32)],
        grid=(n // SEQS,),
        in_specs=[rows(SEG_Q), rows(SEG_K), rows(SEG_V), rows(SEG_Z),
                  pl.BlockSpec((ROWS, BA_LANES), lambda b: (b * SEQS // ROWS, 0)),
                  pl.BlockSpec((SEQS, CONV_W - 1, 3 * SEG), lambda b: (b, 0, 0)),
                  wspec(0), wspec(1), wspec(2), small, small, small, sspec],
        out_specs=[pl.BlockSpec((ROWS, SEG), lambda b: (b * SEQS // ROWS, 0)), sspec],
        compiler_params=_params("arbitrary"),
        name="delta_step",
    )(p, p, p, p, ba, conv_state, w_conv, w_conv, w_conv, alog_row, dt_row, w_onorm, ssm_state)


def _merge_kernel(ya_ref, yb_ref, ga_ref, gb_ref, wa_ref, wb_ref, m_ref):
    pa = jnp.dot(ya_ref[...], wa_ref[...], preferred_element_type=F32)
    pb = jnp.dot(yb_ref[...], wb_ref[...], preferred_element_type=F32)
    m = _sigmoid(ga_ref[...].astype(F32)) * pa + _sigmoid(gb_ref[...].astype(F32)) * pb
    m_ref[...] = m.astype(m_ref.dtype)


def _merge(ya, yb, gates, wa, wb, tm, tn):
    m = ya.shape[0]
    per_seg = SEG // tn
    return pl.pallas_call(
        _merge_kernel,
        out_shape=jax.ShapeDtypeStruct((m, SEG), BF16),
        grid=(per_seg, m // tm),
        in_specs=[pl.BlockSpec((tm, SEG), lambda j, i: (i, 0)),
                  pl.BlockSpec((tm, SEG), lambda j, i: (i, 0)),
                  pl.BlockSpec((tm, tn), lambda j, i: (i, SEG_GA * per_seg + j)),
                  pl.BlockSpec((tm, tn), lambda j, i: (i, SEG_GB * per_seg + j)),
                  pl.BlockSpec((SEG, tn), lambda j, i: (0, j)),
                  pl.BlockSpec((SEG, tn), lambda j, i: (0, j))],
        out_specs=pl.BlockSpec((tm, tn), lambda j, i: (i, j)),
        compiler_params=_params("arbitrary", "arbitrary"),
        name="merge",
    )(ya, yb, gates, gates, wa, wb)


def _out_ln_kernel(x_ref, m_ref, w_ref, g_ref, b_ref, o_ref):
    y = ALPHA * x_ref[...] + jnp.dot(m_ref[...], w_ref[...], preferred_element_type=F32)
    o_ref[...] = _layer_norm(y, g_ref[...], b_ref[...])


def _out_ln(x, m, w, g, b, tm):
    rows = x.shape[0]
    vec = pl.BlockSpec((1, D_MODEL), lambda i: (0, 0))
    return pl.pallas_call(
        _out_ln_kernel,
        out_shape=jax.ShapeDtypeStruct((rows, D_MODEL), F32),
        grid=(rows // tm,),
        in_specs=[pl.BlockSpec((tm, D_MODEL), lambda i: (i, 0)),
                  pl.BlockSpec((tm, SEG), lambda i: (i, 0)),
                  pl.BlockSpec((SEG, D_MODEL), lambda i: (0, 0)),
                  vec, vec],
        out_specs=pl.BlockSpec((tm, D_MODEL), lambda i: (i, 0)),
        compiler_params=_params("arbitrary"),
        name="out_ln",
    )(x, m, w, g, b)


def _ffn_kernel(x_ref, wu_ref, wd_ref, g_ref, b_ref, o_ref, xb_ref):
    f = pl.program_id(1)

    @pl.when(f == 0)
    def _():
        xb_ref[...] = x_ref[...].astype(BF16)
        o_ref[...] = ALPHA * x_ref[...]

    h = jnp.dot(xb_ref[...], wu_ref[...], preferred_element_type=F32)
    h = jnp.square(jnp.maximum(h, 0.0)).astype(BF16)
    o_ref[...] += jnp.dot(h, wd_ref[...], preferred_element_type=F32)

    @pl.when(f == pl.num_programs(1) - 1)
    def _():
        o_ref[...] = _layer_norm(o_ref[...], g_ref[...], b_ref[...])


def _ffn(x, wu, wd, g, b, tm, tf):
    rows = x.shape[0]
    vec = pl.BlockSpec((1, D_MODEL), lambda i, f: (0, 0))
    return pl.pallas_call(
        _ffn_kernel,
        out_shape=jax.ShapeDtypeStruct((rows, D_MODEL), F32),
        grid=(rows // tm, D_FF // tf),
        in_specs=[pl.BlockSpec((tm, D_MODEL), lambda i, f: (i, 0), pipeline_mode=pl.Buffered(1)),
                  pl.BlockSpec((D_MODEL, tf), lambda i, f: (0, f)),
                  pl.BlockSpec((tf, D_MODEL), lambda i, f: (f, 0)),
                  vec, vec],
        out_specs=pl.BlockSpec((tm, D_MODEL), lambda i, f: (i, 0)),
        scratch_shapes=[pltpu.VMEM((tm, D_MODEL), BF16)],
        compiler_params=pltpu.CompilerParams(dimension_semantics=("arbitrary", "arbitrary"),
                                             vmem_limit_bytes=FFN_VMEM_LIMIT),
        name="ffn",
    )(x, wu, wd, g, b)


def _row_tile(m):
    return min(m, 1024)


def _lane_row(vals, offset):
    return jnp.zeros((1, BA_LANES), F32).at[0, offset:offset + vals.shape[0]].set(vals)


def _layer(x, conv_state, ssm_state, wts, batch, seq):
    (w_main, w_gates, w_ba, w_s, b_s, ln_v_g, ln_v_b, w_conv, a_log, dt_bias, w_onorm, w_pa, w_pb, w_o,
     ln1_g, ln1_b, w_up, w_down, ln2_g, ln2_b) = wts
    m = x.shape[0]
    tm = _row_tile(m)
    xb = x.astype(BF16)
    if isinstance(w_main, tuple):
        p, w_main = _inproj(xb, w_main[0], N_MAIN, BF16, 2 * SEG, tm, 1024, layer=w_main[1])
    else:
        p, w_main = _inproj(xb, w_main, N_MAIN, BF16, 2 * SEG, tm, 1024)
    gates, w_gates = _inproj(xb, w_gates, 2 * SEG, BF16, 0, tm, 1024)
    ba, w_ba = _inproj(xb, w_ba, BA_LANES, F32, 0, tm, BA_LANES)

    row = lambda v: v.reshape(1, -1)
    alog_row = _lane_row(a_log, HEADS)
    dt_row = _lane_row(dt_bias, HEADS)
    if conv_state is None:
        (ya,) = _mixer_a(p, w_s, b_s.T, row(ln_v_g), row(ln_v_b), emit_van=False)
        van = None
        yb, ssm_new = _delta_chunks(p, ba, w_conv, alog_row, dt_row, row(w_onorm), batch, seq)
        tail = p.reshape(batch, seq, N_MAIN)[:, seq - (CONV_W - 1):, SEG_Q * SEG:(SEG_V + 1) * SEG]
        conv_new = tail.astype(F32)
    else:
        w_s0 = w_s[:, 0, 0][:, None, None] * jnp.eye(A_CHUNK, dtype=F32)[None]
        b_s0 = jnp.broadcast_to(b_s[:, 0][None, :], (A_CHUNK, GROUPS))
        ya, van = _mixer_a(p, w_s0, b_s0, row(ln_v_g), row(ln_v_b), emit_van=True)
        yb, ssm_new = _delta_step(p, ba, conv_state, ssm_state, w_conv, alog_row, dt_row, row(w_onorm))
        qkv = p[:, SEG_Q * SEG:(SEG_V + 1) * SEG].astype(F32)
        conv_new = jnp.concatenate([conv_state[:, 1:, :], qkv[:, None, :]], axis=1)

    mg = _merge(ya, yb, gates, w_pa, w_pb, tm, 512)
    x1 = _out_ln(x, mg, w_o, row(ln1_g), row(ln1_b), min(m, 512))
    y = _ffn(x1, w_up, w_down, row(ln2_g), row(ln2_b), tm, 512)
    wts = (w_main, w_gates, w_ba) + tuple(wts[3:])
    return (y, van, conv_new, ssm_new), wts


def kernel(x_prompt, x_sample, state_conv, state_ssm, w_in, w_s, b_s, ln_v_g, ln_v_b, w_conv, a_log,
           dt_bias, w_onorm, w_proj_a, w_proj_b, w_o, ln1_g, ln1_b, w_up, w_down, ln2_g, ln2_b):
    depth = w_in.shape[0]
    bp, tp, _ = x_prompt.shape
    bs, ts, _ = x_sample.shape
    assert ts == 1, "the sample group advances one token per sequence"
    yp = x_prompt.reshape(bp * tp, D_MODEL)
    ys = x_sample.reshape(bs, D_MODEL)
    conv_p, ssm_p, vrows_s, conv_s, ssm_s = [], [], [], [], []
    for l in range(depth):
        w_in_t = jnp.swapaxes(w_in, 1, 2)
        w_gates = w_in_t[l, N_MAIN + N_BA:, :]
        w_ba = jnp.pad(w_in_t[l, N_MAIN:N_MAIN + N_BA, :], ((0, BA_LANES - N_BA), (0, 0)))
        wts = ((w_in_t, l), w_gates, w_ba, w_s[l], b_s[l], ln_v_g[l], ln_v_b[l], w_conv[l], a_log[l], dt_bias[l],
               w_onorm[l], w_proj_a[l].astype(BF16), w_proj_b[l].astype(BF16), w_o[l].astype(BF16),
               ln1_g[l], ln1_b[l], w_up[l].astype(BF16), w_down[l].astype(BF16), ln2_g[l], ln2_b[l])
        (yp, _, cp, sp), wts = _layer(yp, None, None, wts, bp, tp)
        (ys, vs, cs, ss), _ = _layer(ys, state_conv[l], state_ssm[l], wts, bs, 1)
        conv_p.append(cp)
        ssm_p.append(sp)
        vrows_s.append(vs.reshape(bs, 1, SEG))
        conv_s.append(cs)
        ssm_s.append(ss)
    return (yp.reshape(bp, tp, D_MODEL), ys.reshape(bs, 1, D_MODEL), jnp.stack(conv_p), jnp.stack(ssm_p),
            jnp.stack(vrows_s), jnp.stack(conv_s), jnp.stack(ssm_s))
```

```python
import functools

import jax
import jax.numpy as jnp
from jax import lax
from jax.experimental import pallas as pl
from jax.experimental.pallas import tpu as pltpu

F32 = jnp.float32
BF16 = jnp.bfloat16

D_MODEL = 2048
HEADS = 16
HEAD_DIM = 128
GROUPS = 16
A_CHUNK = 128
DELTA_CHUNK = 128
CONV_W = 4
D_FF = 4 * D_MODEL
ALPHA = 2.0 ** 0.25
LN_EPS = 1e-5
RMS_EPS = 1e-6

SEG = 2048
N_MAIN = 6 * SEG
(SEG_U, SEG_VA, SEG_Q, SEG_K, SEG_V, SEG_Z) = range(6)
N_BA = 2 * HEADS
(SEG_GA, SEG_GB) = range(2)
HEAD_GROUP = 16
CARRY = 16
BA_LANES = 128
VMEM_LIMIT = 48 * 1024 * 1024
FFN_VMEM_LIMIT = 56 * 1024 * 1024


def _params(*sem):
    return pltpu.CompilerParams(dimension_semantics=sem, vmem_limit_bytes=VMEM_LIMIT)


def _bdot(a, b):
    return jnp.dot(a.astype(BF16), b.astype(BF16), preferred_element_type=F32)


def _gelu(x):
    return 0.5 * x * (1.0 + lax.erf(x * (2.0 ** -0.5)))


def _sigmoid(x):
    return 1.0 / (1.0 + jnp.exp(-x))


def _silu(x):
    return x * _sigmoid(x)


def _softplus(x):
    return jnp.maximum(x, 0.0) + jnp.log1p(jnp.exp(-jnp.abs(x)))


def _layer_norm(x, g, b):
    mu = jnp.mean(x, axis=-1, keepdims=True)
    xc = x - mu
    var = jnp.mean(xc * xc, axis=-1, keepdims=True)
    return xc * lax.rsqrt(var + LN_EPS) * g + b


def _inproj_kernel(x_ref, wt_ref, o_ref, *wb_ref, gelu_panels):
    if wb_ref:
        @pl.when(pl.program_id(1) == 0)
        def _():
            wb_ref[0][...] = wt_ref[...].astype(BF16)
        w = wb_ref[0][...]
    else:
        w = wt_ref[...]
    acc = lax.dot_general(x_ref[...], w, (((1,), (1,)), ((), ())), preferred_element_type=F32)
    j = pl.program_id(0)

    @pl.when(j < gelu_panels)
    def _():
        o_ref[...] = _gelu(acc).astype(o_ref.dtype)

    @pl.when(j >= gelu_panels)
    def _():
        o_ref[...] = acc.astype(o_ref.dtype)


def _inproj(x, wt, n, out_dtype, gelu_cols, tm, tn, layer=None):
    m, k = x.shape
    cast = wt.dtype != BF16
    if layer is None:
        w_spec = pl.BlockSpec((tn, k), lambda j, i: (j, 0))
    else:
        w_spec = pl.BlockSpec((None, tn, k), lambda j, i: (layer, j, 0))
    out_shape = [jax.ShapeDtypeStruct((m, n), out_dtype)]
    out_specs = [pl.BlockSpec((tm, tn), lambda j, i: (i, j))]
    if cast:
        out_shape.append(jax.ShapeDtypeStruct((n, k), BF16))
        out_specs.append(pl.BlockSpec((tn, k), lambda j, i: (j, 0)))
    res = pl.pallas_call(
        functools.partial(_inproj_kernel, gelu_panels=gelu_cols // tn),
        out_shape=out_shape,
        grid=(n // tn, m // tm),
        in_specs=[pl.BlockSpec((tm, k), lambda j, i: (i, 0)), w_spec],
        out_specs=out_specs,
        compiler_params=_params("arbitrary", "arbitrary"),
        name="inproj",
    )(x, wt)
    return (res[0], res[1]) if cast else (res[0], wt)


def _mixer_a_kernel(u_ref, va_ref, ws_ref, bst_ref, g_ref, b_ref, ya_ref, *van_ref):
    van = _layer_norm(va_ref[...].astype(F32), g_ref[...], b_ref[...])
    if van_ref:
        van_ref[0][...] = van
    vb = van.astype(BF16)
    row = lax.broadcasted_iota(jnp.int32, (A_CHUNK, A_CHUNK), 0)
    col = lax.broadcasted_iota(jnp.int32, (A_CHUNK, A_CHUNK), 1)
    causal = row >= col
    for g in range(GROUPS):
        sl = slice(g * A_CHUNK, (g + 1) * A_CHUNK)
        w = jnp.where(causal, ws_ref[g], 0.0).astype(BF16)
        mixed = jnp.dot(w, vb[:, sl], preferred_element_type=F32) + bst_ref[:, g:g + 1]
        ya_ref[:, sl] = (u_ref[:, sl].astype(F32) * mixed).astype(ya_ref.dtype)


def _mixer_a(p, w_s, b_s_t, ln_g, ln_b, emit_van):
    m = p.shape[0]
    out_shape = [jax.ShapeDtypeStruct((m, SEG), BF16)]
    out_specs = [pl.BlockSpec((A_CHUNK, SEG), lambda i: (i, 0))]
    if emit_van:
        out_shape.append(jax.ShapeDtypeStruct((m, SEG), F32))
        out_specs.append(pl.BlockSpec((A_CHUNK, SEG), lambda i: (i, 0)))
    return pl.pallas_call(
        _mixer_a_kernel,
        out_shape=out_shape,
        grid=(m // A_CHUNK,),
        in_specs=[pl.BlockSpec((A_CHUNK, SEG), lambda i: (i, SEG_U)),
                  pl.BlockSpec((A_CHUNK, SEG), lambda i: (i, SEG_VA)),
                  pl.BlockSpec((GROUPS, A_CHUNK, A_CHUNK), lambda i: (0, 0, 0)),
                  pl.BlockSpec((A_CHUNK, GROUPS), lambda i: (0, 0)),
                  pl.BlockSpec((1, SEG), lambda i: (0, 0)),
                  pl.BlockSpec((1, SEG), lambda i: (0, 0))],
        out_specs=out_specs,
        compiler_params=_params("arbitrary"),
        name="mixer_a",
    )(p, p, w_s, b_s_t, ln_g, ln_b)


def _beta_and_g(ba, alog_row, dt_row):
    beta = _sigmoid(ba)
    g = -jnp.exp(alog_row) * _softplus(ba + dt_row)
    return beta, g


def _gated_rms_norm(o, z, w):
    o = o * lax.rsqrt(jnp.mean(o * o, axis=-1, keepdims=True) + RMS_EPS)
    return o * w * _silu(z)


def _l2norm(x):
    return x * lax.rsqrt(jnp.sum(x * x, axis=-1, keepdims=True) + RMS_EPS)


def _unit_lower_inverse(ns, masks):
    eye, blk8, merges = masks
    nd = [jnp.where(blk8, n, 0.0) for n in ns]
    n2 = [_bdot(x, x) for x in nd]
    n4 = [_bdot(x, x) for x in n2]
    p = [eye - x for x in nd]
    p = [x + _bdot(x, y) for x, y in zip(p, n2)]
    p = [x + _bdot(x, y) for x, y in zip(p, n4)]
    for cmask in merges:
        e = [_bdot(jnp.where(cmask, n, 0.0), x) for n, x in zip(ns, p)]
        p = [x - _bdot(x, y) for x, y in zip(p, e)]
    return p


def _inverse_masks():
    c = DELTA_CHUNK
    r = lax.broadcasted_iota(jnp.int32, (c, c), 0)
    q = lax.broadcasted_iota(jnp.int32, (c, c), 1)
    eye = (r == q).astype(F32)
    blk = lambda s: (r >> s) == (q >> s)
    merges = [blk(s + 1) & jnp.logical_not(blk(s)) for s in range(3, c.bit_length() - 1)]
    return (eye, blk(3), merges), r >= q, r > q


def _delta_chunk_kernel(q_ref, k_ref, v_ref, z_ref, ba_ref, wq_ref, wk_ref, wv_ref, alog_ref, dt_ref,
                        won_ref, y_ref, s_out_ref, s_ref, carry_ref):
    c = pl.program_id(1)
    nc = pl.num_programs(1)
    C = DELTA_CHUNK

    @pl.when(c == 0)
    def _():
        s_ref[...] = jnp.zeros_like(s_ref)
        carry_ref[...] = jnp.zeros_like(carry_ref)

    sr = lax.broadcasted_iota(jnp.int32, ((CONV_W - 1) * C, CARRY + C), 0)
    sc = lax.broadcasted_iota(jnp.int32, ((CONV_W - 1) * C, CARRY + C), 1)
    tap = sr >> (C.bit_length() - 1)
    shift = jnp.where(sc - (sr & (C - 1)) - tap == CARRY - (CONV_W - 1), 1.0, 0.0).astype(BF16)

    def conv_silu(x_ref, w_ref, seg):
        cur = x_ref[...]
        ext = jnp.concatenate([carry_ref[seg], cur], axis=0)
        sh = jnp.dot(shift, ext, preferred_element_type=F32)
        acc = cur.astype(F32) * w_ref[CONV_W - 1:CONV_W, :]
        for j in range(CONV_W - 1):
            acc = acc + sh[j * C:(j + 1) * C, :] * w_ref[j:j + 1, :]
        carry_ref[seg] = cur[C - CARRY:C, :]
        return _silu(acc)

    qa = conv_silu(q_ref, wq_ref, 0)
    ka = conv_silu(k_ref, wk_ref, 1)
    va = conv_silu(v_ref, wv_ref, 2)

    beta, g = _beta_and_g(ba_ref[...], alog_ref[...], dt_ref[...])
    row = lax.broadcasted_iota(jnp.int32, (C, BA_LANES), 0)
    gc = g
    for e in range(C.bit_length() - 1):
        gc = gc + jnp.where(row >= (1 << e), pltpu.roll(gc, 1 << e, axis=0), 0.0)
    gc_t = gc.T
    g_last = gc[C - 1:C, :]
    e_gc = jnp.exp(gc)
    e_rest = jnp.exp(g_last - gc)
    e_last = jnp.exp(g_last)

    inv_masks, incl, strict = _inverse_masks()
    w_on = won_ref[...]

    sl = [slice(h * HEAD_DIM, (h + 1) * HEAD_DIM) for h in range(HEADS)]
    col = lambda x, h: x[:, HEADS + h:HEADS + h + 1]
    for first in range(0, HEADS, HEAD_GROUP):
        hs = range(first, first + HEAD_GROUP)
        q = {h: _l2norm(qa[:, sl[h]]) * (HEAD_DIM ** -0.5) for h in hs}
        k = {h: _l2norm(ka[:, sl[h]]) for h in hs}
        kb = {h: k[h] * beta[:, h:h + 1] for h in hs}
        vb = {h: va[:, sl[h]] * beta[:, h:h + 1] for h in hs}
        decay = {h: jnp.exp(jnp.where(incl, col(gc, h) - gc_t[HEADS + h:HEADS + h + 1, :], -jnp.inf))
                 for h in hs}
        kk = {h: lax.dot_general(jnp.concatenate([kb[h], q[h]], axis=0).astype(BF16), k[h].astype(BF16),
                                 (((1,), (1,)), ((), ())), preferred_element_type=F32) for h in hs}
        attn = {h: kk[h][C:] * decay[h] for h in hs}
        t = dict(zip(hs, _unit_lower_inverse(
            [jnp.where(strict, kk[h][:C] * decay[h], 0.0) for h in hs], inv_masks)))
        tr = {h: _bdot(t[h], jnp.concatenate([vb[h], kb[h] * col(e_gc, h)], axis=1)) for h in hs}
        m1 = {h: _bdot(jnp.concatenate([tr[h][:, HEAD_DIM:], q[h] * col(e_gc, h)], axis=0), s_ref[h])
              for h in hs}
        v_new = {h: tr[h][:, :HEAD_DIM] - m1[h][:C] for h in hs}
        o = {h: m1[h][C:] + _bdot(attn[h], v_new[h]) for h in hs}
        kv = {h: lax.dot_general((k[h] * col(e_rest, h)).astype(BF16), v_new[h].astype(BF16),
                                 (((0,), (0,)), ((), ())), preferred_element_type=F32) for h in hs}
        for h in hs:
            s_ref[h] = s_ref[h] * col(e_last, h) + kv[h]
            y_ref[:, sl[h]] = _gated_rms_norm(o[h], z_ref[:, sl[h]].astype(F32), w_on).astype(y_ref.dtype)

    @pl.when(c == nc - 1)
    def _():
        s_out_ref[0] = s_ref[...]


def _delta_chunks(p, ba, w_conv, alog_row, dt_row, w_onorm, batch, seq):
    C = DELTA_CHUNK
    nc = seq // C
    rows = lambda seg: pl.BlockSpec((C, SEG), lambda b, c: (b * nc + c, seg))
    wspec = lambda seg: pl.BlockSpec((CONV_W, SEG), lambda b, c: (0, seg))
    small = pl.BlockSpec((1, BA_LANES), lambda b, c: (0, 0))
    return pl.pallas_call(
        _delta_chunk_kernel,
        out_shape=[jax.ShapeDtypeStruct((batch * seq, SEG), BF16),
                   jax.ShapeDtypeStruct((batch, HEADS, HEAD_DIM, HEAD_DIM), F32)],
        grid=(batch, nc),
        in_specs=[rows(SEG_Q), rows(SEG_K), rows(SEG_V), rows(SEG_Z),
                  pl.BlockSpec((C, BA_LANES), lambda b, c: (b * nc + c, 0)),
                  wspec(0), wspec(1), wspec(2), small, small, small],
        out_specs=[pl.BlockSpec((C, SEG), lambda b, c: (b * nc + c, 0)),
                   pl.BlockSpec((1, HEADS, HEAD_DIM, HEAD_DIM), lambda b, c: (b, 0, 0, 0))],
        scratch_shapes=[pltpu.VMEM((HEADS, HEAD_DIM, HEAD_DIM), F32),
                        pltpu.VMEM((3, CARRY, SEG), BF16)],
        compiler_params=_params("arbitrary", "arbitrary"),
        name="delta_chunks",
    )(p, p, p, p, ba, w_conv, w_conv, w_conv, alog_row, dt_row, w_onorm)


ROWS = 8
SEQS = 4


def _delta_step_kernel(q_ref, k_ref, v_ref, z_ref, ba_ref, cs_ref, wq_ref, wk_ref, wv_ref, alog_ref,
                       dt_ref, won_ref, s_ref, y_ref, s_out_ref):
    part = pl.program_id(0) % (ROWS // SEQS)
    row_id = lax.broadcasted_iota(jnp.int32, (ROWS, 1), 0)
    w_on = won_ref[...]
    pad = jnp.zeros((ROWS - 2, HEAD_DIM), F32)
    zero_row = jnp.zeros((1, HEAD_DIM), F32)
    hs = range(HEADS)
    sl = [slice(h * HEAD_DIM, (h + 1) * HEAD_DIM) for h in hs]

    def one_sequence(i):
        rsel = row_id == part * SEQS + i

        def pick(x):
            return jnp.sum(jnp.where(rsel, x.astype(F32), 0.0), axis=0, keepdims=True)

        def conv_silu(x_ref, w_ref, seg):
            acc = pick(x_ref[...]) * w_ref[CONV_W - 1:CONV_W, :]
            for j in range(CONV_W - 1):
                acc = acc + cs_ref[i, j:j + 1, seg * SEG:(seg + 1) * SEG] * w_ref[j:j + 1, :]
            return _silu(acc)

        qa = conv_silu(q_ref, wq_ref, 0)
        ka = conv_silu(k_ref, wk_ref, 1)
        va = conv_silu(v_ref, wv_ref, 2)
        z = pick(z_ref[...])
        beta, g = _beta_and_g(pick(ba_ref[...]), alog_ref[...], dt_ref[...])
        e_g = jnp.exp(g)
        q = [_l2norm(qa[:, sl[h]]) * (HEAD_DIM ** -0.5) for h in hs]
        k = [_l2norm(ka[:, sl[h]]) for h in hs]
        eg = [e_g[:, HEADS + h:HEADS + h + 1] for h in hs]
        ks_qs = [_bdot(jnp.concatenate([k[h], q[h], pad], axis=0), s_ref[i, h]) for h in hs]
        v_new = [beta[:, h:h + 1] * (va[:, sl[h]] - eg[h] * ks_qs[h][0:1]) for h in hs]
        o = [eg[h] * ks_qs[h][1:2] + jnp.sum(q[h] * k[h], axis=-1, keepdims=True) * v_new[h] for h in hs]
        kv = [lax.dot_general(jnp.concatenate([k[h], pad, zero_row], axis=0).astype(BF16),
                              jnp.broadcast_to(v_new[h], (ROWS, HEAD_DIM)).astype(BF16),
                              (((0,), (0,)), ((), ())), preferred_element_type=F32) for h in hs]
        for h in hs:
            s_out_ref[i, h] = s_ref[i, h] * eg[h] + kv[h]
        outs = [_gated_rms_norm(o[h], z[:, sl[h]], w_on) for h in hs]
        return rsel, jnp.concatenate(outs, axis=1)

    done = [one_sequence(i) for i in range(SEQS)]

    def merged(base):
        for rsel, y_row in done:
            base = jnp.where(rsel, y_row, base)
        return base.astype(y_ref.dtype)

    @pl.when(part == 0)
    def _():
        y_ref[...] = merged(jnp.broadcast_to(done[0][1], y_ref.shape))

    @pl.when(part != 0)
    def _():
        y_ref[...] = merged(y_ref[...].astype(F32))


def _delta_step(p, ba, conv_state, ssm_state, w_conv, alog_row, dt_row, w_onorm):
    n = p.shape[0]
    rows = lambda seg: pl.BlockSpec((ROWS, SEG), lambda b: (b * SEQS // ROWS, seg))
    wspec = lambda seg: pl.BlockSpec((CONV_W, SEG), lambda b: (0, seg))
    small = pl.BlockSpec((1, BA_LANES), lambda b: (0, 0))
    sspec = pl.BlockSpec((SEQS, HEADS, HEAD_DIM, HEAD_DIM), lambda b: (b, 0, 0, 0))
    return pl.pallas_call(
        _delta_step_kernel,
        out_shape=[jax.ShapeDtypeStruct((n, SEG), BF16),
                   jax.ShapeDtypeStruct(ssm_state.shape, F32)],
        grid=(n // SEQS,),
        in_specs=[rows(SEG_Q), rows(SEG_K), rows(SEG_V), rows(SEG_Z),
                  pl.BlockSpec((ROWS, BA_LANES), lambda b: (b * SEQS // ROWS, 0)),
                  pl.BlockSpec((SEQS, CONV_W - 1, 3 * SEG), lambda b: (b, 0, 0)),
                  wspec(0), wspec(1), wspec(2), small, small, small, sspec],
        out_specs=[pl.BlockSpec((ROWS, SEG), lambda b: (b * SEQS // ROWS, 0)), sspec],
        compiler_params=_params("arbitrary"),
        name="delta_step",
    )(p, p, p, p, ba, conv_state, w_conv, w_conv, w_conv, alog_row, dt_row, w_onorm, ssm_state)


def _merge_kernel(ya_ref, yb_ref, ga_ref, gb_ref, wa_ref, wb_ref, m_ref):
    pa = jnp.dot(ya_ref[...], wa_ref[...], preferred_element_type=F32)
    pb = jnp.dot(yb_ref[...], wb_ref[...], preferred_element_type=F32)
    m = _sigmoid(ga_ref[...].astype(F32)) * pa + _sigmoid(gb_ref[...].astype(F32)) * pb
    m_ref[...] = m.astype(m_ref.dtype)


def _merge(ya, yb, gates, wa, wb, tm, tn):
    m = ya.shape[0]
    per_seg = SEG // tn
    return pl.pallas_call(
        _merge_kernel,
        out_shape=jax.ShapeDtypeStruct((m, SEG), BF16),
        grid=(per_seg, m // tm),
        in_specs=[pl.BlockSpec((tm, SEG), lambda j, i: (i, 0)),
                  pl.BlockSpec((tm, SEG), lambda j, i: (i, 0)),
                  pl.BlockSpec((tm, tn), lambda j, i: (i, SEG_GA * per_seg + j)),
                  pl.BlockSpec((tm, tn), lambda j, i: (i, SEG_GB * per_seg + j)),
                  pl.BlockSpec((SEG, tn), lambda j, i: (0, j)),
                  pl.BlockSpec((SEG, tn), lambda j, i: (0, j))],
        out_specs=pl.BlockSpec((tm, tn), lambda j, i: (i, j)),
        compiler_params=_params("arbitrary", "arbitrary"),
        name="merge",
    )(ya, yb, gates, gates, wa, wb)


def _out_ln_kernel(x_ref, m_ref, w_ref, g_ref, b_ref, o_ref):
    y = ALPHA * x_ref[...] + jnp.dot(m_ref[...], w_ref[...], preferred_element_type=F32)
    o_ref[...] = _layer_norm(y, g_ref[...], b_ref[...])


def _out_ln(x, m, w, g, b, tm):
    rows = x.shape[0]
    vec = pl.BlockSpec((1, D_MODEL), lambda i: (0, 0))
    return pl.pallas_call(
        _out_ln_kernel,
        out_shape=jax.ShapeDtypeStruct((rows, D_MODEL), F32),
        grid=(rows // tm,),
        in_specs=[pl.BlockSpec((tm, D_MODEL), lambda i: (i, 0)),
                  pl.BlockSpec((tm, SEG), lambda i: (i, 0)),
                  pl.BlockSpec((SEG, D_MODEL), lambda i: (0, 0)),
                  vec, vec],
        out_specs=pl.BlockSpec((tm, D_MODEL), lambda i: (i, 0)),
        compiler_params=_params("arbitrary"),
        name="out_ln",
    )(x, m, w, g, b)


def _ffn_kernel(x_ref, wu_ref, wd_ref, g_ref, b_ref, o_ref, xb_ref):
    f = pl.program_id(1)

    @pl.when(f == 0)
    def _():
        xb_ref[...] = x_ref[...].astype(BF16)
        o_ref[...] = ALPHA * x_ref[...]

    h = jnp.dot(xb_ref[...], wu_ref[...], preferred_element_type=F32)
    h = jnp.square(jnp.maximum(h, 0.0)).astype(BF16)
    o_ref[...] += jnp.dot(h, wd_ref[...], preferred_element_type=F32)

    @pl.when(f == pl.num_programs(1) - 1)
    def _():
        o_ref[...] = _layer_norm(o_ref[...], g_ref[...], b_ref[...])


def _ffn(x, wu, wd, g, b, tm, tf):
    rows = x.shape[0]
    vec = pl.BlockSpec((1, D_MODEL), lambda i, f: (0, 0))
    return pl.pallas_call(
        _ffn_kernel,
        out_shape=jax.ShapeDtypeStruct((rows, D_MODEL), F32),
        grid=(rows // tm, D_FF // tf),
        in_specs=[pl.BlockSpec((tm, D_MODEL), lambda i, f: (i, 0), pipeline_mode=pl.Buffered(1)),
                  pl.BlockSpec((D_MODEL, tf), lambda i, f: (0, f)),
                  pl.BlockSpec((tf, D_MODEL), lambda i, f: (f, 0)),
                  vec, vec],
        out_specs=pl.BlockSpec((tm, D_MODEL), lambda i, f: (i, 0)),
        scratch_shapes=[pltpu.VMEM((tm, D_MODEL), BF16)],
        compiler_params=pltpu.CompilerParams(dimension_semantics=("arbitrary", "arbitrary"),
                                             vmem_limit_bytes=FFN_VMEM_LIMIT),
        name="ffn",
    )(x, wu, wd, g, b)


def _row_tile(m):
    return min(m, 1024)


def _lane_row(vals, offset):
    return jnp.zeros((1, BA_LANES), F32).at[0, offset:offset + vals.shape[0]].set(vals)


def _layer(x, conv_state, ssm_state, wts, batch, seq):
    (w_main, w_gates, w_ba, w_s, b_s, ln_v_g, ln_v_b, w_conv, a_log, dt_bias, w_onorm, w_pa, w_pb, w_o,
     ln1_g, ln1_b, w_up, w_down, ln2_g, ln2_b) = wts
    m = x.shape[0]
    tm = _row_tile(m)
    xb = x.astype(BF16)
    if isinstance(w_main, tuple):
        p, w_main = _inproj(xb, w_main[0], N_MAIN, BF16, 2 * SEG, tm, 1024, layer=w_main[1])
    else:
        p, w_main = _inproj(xb, w_main, N_MAIN, BF16, 2 * SEG, tm, 1024)
    gates, w_gates = _inproj(xb, w_gates, 2 * SEG, BF16, 0, tm, 1024)
    ba, w_ba = _inproj(xb, w_ba, BA_LANES, F32, 0, tm, BA_LANES)

    row = lambda v: v.reshape(1, -1)
    alog_row = _lane_row(a_log, HEADS)
    dt_row = _lane_row(dt_bias, HEADS)
    if conv_state is None:
        (ya,) = _mixer_a(p, w_s, b_s.T, row(ln_v_g), row(ln_v_b), emit_van=False)
        van = None
        yb, ssm_new = _delta_chunks(p, ba, w_conv, alog_row, dt_row, row(w_onorm), batch, seq)
        tail = p.reshape(batch, seq, N_MAIN)[:, seq - (CONV_W - 1):, SEG_Q * SEG:(SEG_V + 1) * SEG]
        conv_new = tail.astype(F32)
    else:
        w_s0 = w_s[:, 0, 0][:, None, None] * jnp.eye(A_CHUNK, dtype=F32)[None]
        b_s0 = jnp.broadcast_to(b_s[:, 0][None, :], (A_CHUNK, GROUPS))
        ya, van = _mixer_a(p, w_s0, b_s0, row(ln_v_g), row(ln_v_b), emit_van=True)
        yb, ssm_new = _delta_step(p, ba, conv_state, ssm_state, w_conv, alog_row, dt_row, row(w_onorm))
        qkv = p[:, SEG_Q * SEG:(SEG_V + 1) * SEG].astype(F32)
        conv_new = jnp.concatenate([conv_state[:, 1:, :], qkv[:, None, :]], axis=1)

    mg = _merge(ya, yb, gates, w_pa, w_pb, tm, 512)
    x1 = _out_ln(x, mg, w_o, row(ln1_g), row(ln1_b), min(m, 512))
    y = _ffn(x1, w_up, w_down, row(ln2_g), row(ln2_b), tm, 512)
    wts = (w_main, w_gates, w_ba) + tuple(wts[3:])
    return (y, van, conv_new, ssm_new), wts


def kernel(x_prompt, x_sample, state_conv, state_ssm, w_in, w_s, b_s, ln_v_g, ln_v_b, w_conv, a_log,
           dt_bias, w_onorm, w_proj_a, w_proj_b, w_o, ln1_g, ln1_b, w_up, w_down, ln2_g, ln2_b):
    depth = w_in.shape[0]
    bp, tp, _ = x_prompt.shape
    bs, ts, _ = x_sample.shape
    assert ts == 1, "the sample group advances one token per sequence"
    yp = x_prompt.reshape(bp * tp, D_MODEL)
    ys = x_sample.reshape(bs, D_MODEL)
    conv_p, ssm_p, vrows_s, conv_s, ssm_s = [], [], [], [], []
    for l in range(depth):
        w_in_t = jnp.swapaxes(w_in, 1, 2)
        w_gates = w_in_t[l, N_MAIN + N_BA:, :]
        w_ba = jnp.pad(w_in_t[l, N_MAIN:N_MAIN + N_BA, :], ((0, BA_LANES - N_BA), (0, 0)))
        wts = ((w_in_t, l), w_gates, w_ba, w_s[l], b_s[l], ln_v_g[l], ln_v_b[l], w_conv[l], a_log[l], dt_bias[l],
               w_onorm[l], w_proj_a[l].astype(BF16), w_proj_b[l].astype(BF16), w_o[l].astype(BF16),
               ln1_g[l], ln1_b[l], w_up[l].astype(BF16), w_down[l].astype(BF16), ln2_g[l], ln2_b[l])
        (yp, _, cp, sp), wts = _layer(yp, None, None, wts, bp, tp)
        (ys, vs, cs, ss), _ = _layer(ys, state_conv[l], state_ssm[l], wts, bs, 1)
        conv_p.append(cp)
        ssm_p.append(sp)
        vrows_s.append(vs.reshape(bs, 1, SEG))
        conv_s.append(cs)
        ssm_s.append(ss)
    return (yp.reshape(bp, tp, D_MODEL), ys.reshape(bs, 1, D_MODEL), jnp.stack(conv_p), jnp.stack(ssm_p),
            jnp.stack(vrows_s), jnp.stack(conv_s), jnp.stack(ssm_s))
```

```python
import functools

import jax
import jax.numpy as jnp
from jax import lax
from jax.experimental import pallas as pl
from jax.experimental.pallas import tpu as pltpu

F32 = jnp.float32
BF16 = jnp.bfloat16

D_MODEL = 2048
HEADS = 16
HEAD_DIM = 128
GROUPS = 16
A_CHUNK = 128
A_ROWS = 4 * A_CHUNK
DELTA_CHUNK = 128
CONV_W = 4
D_FF = 4 * D_MODEL
ALPHA = 2.0 ** 0.25
LN_EPS = 1e-5
RMS_EPS = 1e-6

SEG = 2048
N_MAIN = 6 * SEG
(SEG_U, SEG_VA, SEG_Q, SEG_K, SEG_V, SEG_Z) = range(6)
N_BA = 2 * HEADS
(SEG_GA, SEG_GB) = range(2)
HEAD_GROUP = 16
CARRY = 16
BA_LANES = 128
VMEM_LIMIT = 48 * 1024 * 1024
FFN_VMEM_LIMIT = 56 * 1024 * 1024


def _params(*sem):
    return pltpu.CompilerParams(dimension_semantics=sem, vmem_limit_bytes=VMEM_LIMIT)


def _bdot(a, b):
    return jnp.dot(a.astype(BF16), b.astype(BF16), preferred_element_type=F32)


def _gelu(x):
    return 0.5 * x * (1.0 + lax.erf(x * (2.0 ** -0.5)))


def _sigmoid(x):
    return 1.0 / (1.0 + jnp.exp(-x))


def _silu(x):
    return x * _sigmoid(x)


def _softplus(x):
    return jnp.maximum(x, 0.0) + jnp.log1p(jnp.exp(-jnp.abs(x)))


def _layer_norm(x, g, b):
    mu = jnp.mean(x, axis=-1, keepdims=True)
    xc = x - mu
    var = jnp.mean(xc * xc, axis=-1, keepdims=True)
    return xc * lax.rsqrt(var + LN_EPS) * g + b


def _inproj_kernel(x_ref, wt_ref, o_ref, *wb_ref, gelu_panels):
    if wb_ref:
        @pl.when(pl.program_id(1) == 0)
        def _():
            wb_ref[0][...] = wt_ref[...].astype(BF16)
        w = wb_ref[0][...]
    else:
        w = wt_ref[...]
    acc = lax.dot_general(x_ref[...], w, (((1,), (1,)), ((), ())), preferred_element_type=F32)
    j = pl.program_id(0)

    @pl.when(j < gelu_panels)
    def _():
        o_ref[...] = _gelu(acc).astype(o_ref.dtype)

    @pl.when(j >= gelu_panels)
    def _():
        o_ref[...] = acc.astype(o_ref.dtype)


def _inproj(x, wt, n, out_dtype, gelu_cols, tm, tn, layer=None):
    m, k = x.shape
    cast = wt.dtype != BF16
    if layer is None:
        w_spec = pl.BlockSpec((tn, k), lambda j, i: (j, 0))
    else:
        w_spec = pl.BlockSpec((None, tn, k), lambda j, i: (layer, j, 0))
    out_shape = [jax.ShapeDtypeStruct((m, n), out_dtype)]
    out_specs = [pl.BlockSpec((tm, tn), lambda j, i: (i, j))]
    if cast:
        out_shape.append(jax.ShapeDtypeStruct((n, k), BF16))
        out_specs.append(pl.BlockSpec((tn, k), lambda j, i: (j, 0)))
    res = pl.pallas_call(
        functools.partial(_inproj_kernel, gelu_panels=gelu_cols // tn),
        out_shape=out_shape,
        grid=(n // tn, m // tm),
        in_specs=[pl.BlockSpec((tm, k), lambda j, i: (i, 0)), w_spec],
        out_specs=out_specs,
        compiler_params=_params("arbitrary", "arbitrary"),
        name="inproj",
    )(x, wt)
    return (res[0], res[1]) if cast else (res[0], wt)


def _mixer_a_kernel(u_ref, va_ref, ws_ref, bst_ref, g_ref, b_ref, ya_ref, *van_ref):
    van = _layer_norm(va_ref[...].astype(F32), g_ref[...], b_ref[...])
    if van_ref:
        van_ref[0][...] = van
    vb = van.astype(BF16)
    chunks = [slice(c * A_CHUNK, (c + 1) * A_CHUNK) for c in range(vb.shape[0] // A_CHUNK)]
    row = lax.broadcasted_iota(jnp.int32, (A_CHUNK, A_CHUNK), 0)
    col = lax.broadcasted_iota(jnp.int32, (A_CHUNK, A_CHUNK), 1)
    causal = row >= col
    for g in range(GROUPS):
        sl = slice(g * A_CHUNK, (g + 1) * A_CHUNK)
        w = jnp.where(causal, ws_ref[g], 0.0).astype(BF16)
        v = jnp.concatenate([vb[rows, sl] for rows in chunks], axis=1)
        mixed = jnp.dot(w, v, preferred_element_type=F32)
        for c, rows in enumerate(chunks):
            m_c = mixed[:, c * A_CHUNK:(c + 1) * A_CHUNK] + bst_ref[:, g:g + 1]
            ya_ref[rows, sl] = (u_ref[rows, sl].astype(F32) * m_c).astype(ya_ref.dtype)


def _mixer_a(p, w_s, b_s_t, ln_g, ln_b, emit_van):
    m = p.shape[0]
    rows = min(m, A_ROWS)
    out_shape = [jax.ShapeDtypeStruct((m, SEG), BF16)]
    out_specs = [pl.BlockSpec((rows, SEG), lambda i: (i, 0))]
    if emit_van:
        out_shape.append(jax.ShapeDtypeStruct((m, SEG), F32))
        out_specs.append(pl.BlockSpec((rows, SEG), lambda i: (i, 0)))
    return pl.pallas_call(
        _mixer_a_kernel,
        out_shape=out_shape,
        grid=(m // rows,),
        in_specs=[pl.BlockSpec((rows, SEG), lambda i: (i, SEG_U)),
                  pl.BlockSpec((rows, SEG), lambda i: (i, SEG_VA)),
                  pl.BlockSpec((GROUPS, A_CHUNK, A_CHUNK), lambda i: (0, 0, 0)),
                  pl.BlockSpec((A_CHUNK, GROUPS), lambda i: (0, 0)),
                  pl.BlockSpec((1, SEG), lambda i: (0, 0)),
                  pl.BlockSpec((1, SEG), lambda i: (0, 0))],
        out_specs=out_specs,
        compiler_params=_params("arbitrary"),
        name="mixer_a",
    )(p, p, w_s, b_s_t, ln_g, ln_b)


def _beta_and_g(ba, alog_row, dt_row):
    beta = _sigmoid(ba)
    g = -jnp.exp(alog_row) * _softplus(ba + dt_row)
    return beta, g


def _gated_rms_norm(o, z, w):
    o = o * lax.rsqrt(jnp.mean(o * o, axis=-1, keepdims=True) + RMS_EPS)
    return o * w * _silu(z)


def _l2norm(x):
    return x * lax.rsqrt(jnp.sum(x * x, axis=-1, keepdims=True) + RMS_EPS)


def _unit_lower_inverse(ns, masks):
    eye, blk8, merges = masks
    nd = [jnp.where(blk8, n, 0.0) for n in ns]
    n2 = [_bdot(x, x) for x in nd]
    n4 = [_bdot(x, x) for x in n2]
    p = [eye - x for x in nd]
    p = [x + _bdot(x, y) for x, y in zip(p, n2)]
    p = [x + _bdot(x, y) for x, y in zip(p, n4)]
    for cmask in merges:
        e = [_bdot(jnp.where(cmask, n, 0.0), x) for n, x in zip(ns, p)]
        p = [x - _bdot(x, y) for x, y in zip(p, e)]
    return p


def _inverse_masks():
    c = DELTA_CHUNK
    r = lax.broadcasted_iota(jnp.int32, (c, c), 0)
    q = lax.broadcasted_iota(jnp.int32, (c, c), 1)
    eye = (r == q).astype(F32)
    blk = lambda s: (r >> s) == (q >> s)
    merges = [blk(s + 1) & jnp.logical_not(blk(s)) for s in range(3, c.bit_length() - 1)]
    return (eye, blk(3), merges), r >= q, r > q


def _delta_chunk_kernel(q_ref, k_ref, v_ref, z_ref, ba_ref, wq_ref, wk_ref, wv_ref, alog_ref, dt_ref,
                        won_ref, y_ref, s_out_ref, s_ref, carry_ref):
    c = pl.program_id(1)
    nc = pl.num_programs(1)
    C = DELTA_CHUNK

    @pl.when(c == 0)
    def _():
        s_ref[...] = jnp.zeros_like(s_ref)
        carry_ref[...] = jnp.zeros_like(carry_ref)

    sr = lax.broadcasted_iota(jnp.int32, ((CONV_W - 1) * C, CARRY + C), 0)
    sc = lax.broadcasted_iota(jnp.int32, ((CONV_W - 1) * C, CARRY + C), 1)
    tap = sr >> (C.bit_length() - 1)
    shift = jnp.where(sc - (sr & (C - 1)) - tap == CARRY - (CONV_W - 1), 1.0, 0.0).astype(BF16)

    def conv_silu(x_ref, w_ref, seg):
        cur = x_ref[...]
        ext = jnp.concatenate([carry_ref[seg], cur], axis=0)
        sh = jnp.dot(shift, ext, preferred_element_type=F32)
        acc = cur.astype(F32) * w_ref[CONV_W - 1:CONV_W, :]
        for j in range(CONV_W - 1):
            acc = acc + sh[j * C:(j + 1) * C, :] * w_ref[j:j + 1, :]
        carry_ref[seg] = cur[C - CARRY:C, :]
        return _silu(acc)

    qa = conv_silu(q_ref, wq_ref, 0)
    ka = conv_silu(k_ref, wk_ref, 1)
    va = conv_silu(v_ref, wv_ref, 2)

    beta, g = _beta_and_g(ba_ref[...], alog_ref[...], dt_ref[...])
    row = lax.broadcasted_iota(jnp.int32, (C, BA_LANES), 0)
    gc = g
    for e in range(C.bit_length() - 1):
        gc = gc + jnp.where(row >= (1 << e), pltpu.roll(gc, 1 << e, axis=0), 0.0)
    gc_t = gc.T
    g_last = gc[C - 1:C, :]
    e_gc = jnp.exp(gc)
    e_rest = jnp.exp(g_last - gc)
    e_last = jnp.exp(g_last)

    inv_masks, incl, strict = _inverse_masks()
    w_on = won_ref[...]

    sl = [slice(h * HEAD_DIM, (h + 1) * HEAD_DIM) for h in range(HEADS)]
    col = lambda x, h: x[:, HEADS + h:HEADS + h + 1]
    for first in range(0, HEADS, HEAD_GROUP):
        hs = range(first, first + HEAD_GROUP)
        q = {h: _l2norm(qa[:, sl[h]]) * (HEAD_DIM ** -0.5) for h in hs}
        k = {h: _l2norm(ka[:, sl[h]]) for h in hs}
        kb = {h: k[h] * beta[:, h:h + 1] for h in hs}
        vb = {h: va[:, sl[h]] * beta[:, h:h + 1] for h in hs}
        decay = {h: jnp.exp(jnp.where(incl, col(gc, h) - gc_t[HEADS + h:HEADS + h + 1, :], -jnp.inf))
                 for h in hs}
        kk = {h: lax.dot_general(jnp.concatenate([kb[h], q[h]], axis=0).astype(BF16), k[h].astype(BF16),
                                 (((1,), (1,)), ((), ())), preferred_element_type=F32) for h in hs}
        attn = {h: kk[h][C:] * decay[h] for h in hs}
        t = dict(zip(hs, _unit_lower_inverse(
            [jnp.where(strict, kk[h][:C] * decay[h], 0.0) for h in hs], inv_masks)))
        tr = {h: _bdot(t[h], jnp.concatenate([vb[h], kb[h] * col(e_gc, h)], axis=1)) for h in hs}
        m1 = {h: _bdot(jnp.concatenate([tr[h][:, HEAD_DIM:], q[h] * col(e_gc, h)], axis=0), s_ref[h])
              for h in hs}
        v_new = {h: tr[h][:, :HEAD_DIM] - m1[h][:C] for h in hs}
        o = {h: m1[h][C:] + _bdot(attn[h], v_new[h]) for h in hs}
        kv = {h: lax.dot_general((k[h] * col(e_rest, h)).astype(BF16), v_new[h].astype(BF16),
                                 (((0,), (0,)), ((), ())), preferred_element_type=F32) for h in hs}
        for h in hs:
            s_ref[h] = s_ref[h] * col(e_last, h) + kv[h]
            y_ref[:, sl[h]] = _gated_rms_norm(o[h], z_ref[:, sl[h]].astype(F32), w_on).astype(y_ref.dtype)

    @pl.when(c == nc - 1)
    def _():
        s_out_ref[0] = s_ref[...]


def _delta_chunks(p, ba, w_conv, alog_row, dt_row, w_onorm, batch, seq):
    C = DELTA_CHUNK
    nc = seq // C
    rows = lambda seg: pl.BlockSpec((C, SEG), lambda b, c: (b * nc + c, seg))
    wspec = lambda seg: pl.BlockSpec((CONV_W, SEG), lambda b, c: (0, seg))
    small = pl.BlockSpec((1, BA_LANES), lambda b, c: (0, 0))
    return pl.pallas_call(
        _delta_chunk_kernel,
        out_shape=[jax.ShapeDtypeStruct((batch * seq, SEG), BF16),
                   jax.ShapeDtypeStruct((batch, HEADS, HEAD_DIM, HEAD_DIM), F32)],
        grid=(batch, nc),
        in_specs=[rows(SEG_Q), rows(SEG_K), rows(SEG_V), rows(SEG_Z),
                  pl.BlockSpec((C, BA_LANES), lambda b, c: (b * nc + c, 0)),
                  wspec(0), wspec(1), wspec(2), small, small, small],
        out_specs=[pl.BlockSpec((C, SEG), lambda b, c: (b * nc + c, 0)),
                   pl.BlockSpec((1, HEADS, HEAD_DIM, HEAD_DIM), lambda b, c: (b, 0, 0, 0))],
        scratch_shapes=[pltpu.VMEM((HEADS, HEAD_DIM, HEAD_DIM), F32),
                        pltpu.VMEM((3, CARRY, SEG), BF16)],
        compiler_params=_params("arbitrary", "arbitrary"),
        name="delta_chunks",
    )(p, p, p, p, ba, w_conv, w_conv, w_conv, alog_row, dt_row, w_onorm)


ROWS = 8
SEQS = 8


def _delta_step_kernel(q_ref, k_ref, v_ref, z_ref, ba_ref, cs_ref, wq_ref, wk_ref, wv_ref, alog_ref,
                       dt_ref, won_ref, s_ref, y_ref, s_out_ref):
    part = pl.program_id(0) % (ROWS // SEQS)
    row_id = lax.broadcasted_iota(jnp.int32, (ROWS, 1), 0)
    w_on = won_ref[...]
    pad = jnp.zeros((ROWS - 2, HEAD_DIM), F32)
    zero_row = jnp.zeros((1, HEAD_DIM), F32)
    hs = range(HEADS)
    sl = [slice(h * HEAD_DIM, (h + 1) * HEAD_DIM) for h in hs]

    def one_sequence(i):
        rsel = row_id == part * SEQS + i

        def pick(x):
            return jnp.sum(jnp.where(rsel, x.astype(F32), 0.0), axis=0, keepdims=True)

        def conv_silu(x_ref, w_ref, seg):
            acc = pick(x_ref[...]) * w_ref[CONV_W - 1:CONV_W, :]
            for j in range(CONV_W - 1):
                acc = acc + cs_ref[i, j:j + 1, seg * SEG:(seg + 1) * SEG] * w_ref[j:j + 1, :]
            return _silu(acc)

        qa = conv_silu(q_ref, wq_ref, 0)
        ka = conv_silu(k_ref, wk_ref, 1)
        va = conv_silu(v_ref, wv_ref, 2)
        z = pick(z_ref[...])
        beta, g = _beta_and_g(pick(ba_ref[...]), alog_ref[...], dt_ref[...])
        e_g = jnp.exp(g)
        q = [_l2norm(qa[:, sl[h]]) * (HEAD_DIM ** -0.5) for h in hs]
        k = [_l2norm(ka[:, sl[h]]) for h in hs]
        eg = [e_g[:, HEADS + h:HEADS + h + 1] for h in hs]
        ks_qs = [_bdot(jnp.concatenate([k[h], q[h], pad], axis=0), s_ref[i, h]) for h in hs]
        v_new = [beta[:, h:h + 1] * (va[:, sl[h]] - eg[h] * ks_qs[h][0:1]) for h in hs]
        o = [eg[h] * ks_qs[h][1:2] + jnp.sum(q[h] * k[h], axis=-1, keepdims=True) * v_new[h] for h in hs]
        kv = [lax.dot_general(jnp.concatenate([k[h], pad, zero_row], axis=0).astype(BF16),
                              jnp.broadcast_to(v_new[h], (ROWS, HEAD_DIM)).astype(BF16),
                              (((0,), (0,)), ((), ())), preferred_element_type=F32) for h in hs]
        for h in hs:
            s_out_ref[i, h] = s_ref[i, h] * eg[h] + kv[h]
        outs = [_gated_rms_norm(o[h], z[:, sl[h]], w_on) for h in hs]
        return rsel, jnp.concatenate(outs, axis=1)

    done = [one_sequence(i) for i in range(SEQS)]

    def merged(base):
        for rsel, y_row in done:
            base = jnp.where(rsel, y_row, base)
        return base.astype(y_ref.dtype)

    @pl.when(part == 0)
    def _():
        y_ref[...] = merged(jnp.broadcast_to(done[0][1], y_ref.shape))

    @pl.when(part != 0)
    def _():
        y_ref[...] = merged(y_ref[...].astype(F32))


def _delta_step(p, ba, conv_state, ssm_state, w_conv, alog_row, dt_row, w_onorm):
    n = p.shape[0]
    rows = lambda seg: pl.BlockSpec((ROWS, SEG), lambda b: (b * SEQS // ROWS, seg))
    wspec = lambda seg: pl.BlockSpec((CONV_W, SEG), lambda b: (0, seg))
    small = pl.BlockSpec((1, BA_LANES), lambda b: (0, 0))
    sspec = pl.BlockSpec((SEQS, HEADS, HEAD_DIM, HEAD_DIM), lambda b: (b, 0, 0, 0))
    return pl.pallas_call(
        _delta_step_kernel,
        out_shape=[jax.ShapeDtypeStruct((n, SEG), BF16),
                   jax.ShapeDtypeStruct(ssm_state.shape, F32)],
        grid=(n // SEQS,),
        in_specs=[rows(SEG_Q), rows(SEG_K), rows(SEG_V), rows(SEG_Z),
                  pl.BlockSpec((ROWS, BA_LANES), lambda b: (b * SEQS // ROWS, 0)),
                  pl.BlockSpec((SEQS, CONV_W - 1, 3 * SEG), lambda b: (b, 0, 0)),
                  wspec(0), wspec(1), wspec(2), small, small, small, sspec],
        out_specs=[pl.BlockSpec((ROWS, SEG), lambda b: (b * SEQS // ROWS, 0)), sspec],
        compiler_params=_params("arbitrary"),
        name="delta_step",
    )(p, p, p, p, ba, conv_state, w_conv, w_conv, w_conv, alog_row, dt_row, w_onorm, ssm_state)


def _merge_kernel(ya_ref, yb_ref, ga_ref, gb_ref, wa_ref, wb_ref, *rest):
    m_ref = rest[len(rest) // 2]
    pa = jnp.dot(ya_ref[...], wa_ref[...], preferred_element_type=F32)
    pb = jnp.dot(yb_ref[...], wb_ref[...], preferred_element_type=F32)
    m = _sigmoid(ga_ref[...].astype(F32)) * pa + _sigmoid(gb_ref[...].astype(F32)) * pb
    m_ref[...] = m.astype(m_ref.dtype)
    n_side = len(rest) // 2
    for src_ref, dst_ref in zip(rest[:n_side], rest[n_side + 1:]):
        dst_ref[...] = src_ref[...].astype(BF16)


def _merge(ya, yb, gates, wa, wb, tm, tn, side=()):
    m = ya.shape[0]
    per_seg = SEG // tn
    n_i = m // tm
    steps = per_seg * n_i
    slab = lambda w: pl.BlockSpec((w.shape[0] // steps, w.shape[1]), lambda j, i: (j * n_i + i, 0))
    return pl.pallas_call(
        _merge_kernel,
        out_shape=[jax.ShapeDtypeStruct((m, SEG), BF16)] + [jax.ShapeDtypeStruct(w.shape, BF16) for w in side],
        grid=(per_seg, n_i),
        in_specs=[pl.BlockSpec((tm, SEG), lambda j, i: (i, 0)),
                  pl.BlockSpec((tm, SEG), lambda j, i: (i, 0)),
                  pl.BlockSpec((tm, tn), lambda j, i: (i, SEG_GA * per_seg + j)),
                  pl.BlockSpec((tm, tn), lambda j, i: (i, SEG_GB * per_seg + j)),
                  pl.BlockSpec((SEG, tn), lambda j, i: (0, j)),
                  pl.BlockSpec((SEG, tn), lambda j, i: (0, j))] + [slab(w) for w in side],
        out_specs=[pl.BlockSpec((tm, tn), lambda j, i: (i, j))] + [slab(w) for w in side],
        compiler_params=_params("arbitrary", "arbitrary"),
        name="merge",
    )(ya, yb, gates, gates, wa, wb, *side)


def _out_ln_kernel(x_ref, m_ref, w_ref, g_ref, b_ref, o_ref):
    y = ALPHA * x_ref[...] + jnp.dot(m_ref[...], w_ref[...], preferred_element_type=F32)
    o_ref[...] = _layer_norm(y, g_ref[...], b_ref[...])


def _out_ln(x, m, w, g, b, tm):
    rows = x.shape[0]
    vec = pl.BlockSpec((1, D_MODEL), lambda i: (0, 0))
    return pl.pallas_call(
        _out_ln_kernel,
        out_shape=jax.ShapeDtypeStruct((rows, D_MODEL), F32),
        grid=(rows // tm,),
        in_specs=[pl.BlockSpec((tm, D_MODEL), lambda i: (i, 0)),
                  pl.BlockSpec((tm, SEG), lambda i: (i, 0)),
                  pl.BlockSpec((SEG, D_MODEL), lambda i: (0, 0)),
                  vec, vec],
        out_specs=pl.BlockSpec((tm, D_MODEL), lambda i: (i, 0)),
        compiler_params=_params("arbitrary"),
        name="out_ln",
    )(x, m, w, g, b)


def _ffn_kernel(x_ref, wu_ref, wd_ref, g_ref, b_ref, o_ref, xb_ref):
    f = pl.program_id(1)

    @pl.when(f == 0)
    def _():
        xb_ref[...] = x_ref[...].astype(BF16)
        o_ref[...] = ALPHA * x_ref[...]

    h = jnp.dot(xb_ref[...], wu_ref[...], preferred_element_type=F32)
    h = jnp.square(jnp.maximum(h, 0.0)).astype(BF16)
    o_ref[...] += jnp.dot(h, wd_ref[...], preferred_element_type=F32)

    @pl.when(f == pl.num_programs(1) - 1)
    def _():
        o_ref[...] = _layer_norm(o_ref[...], g_ref[...], b_ref[...])


def _ffn(x, wu, wd, g, b, tm, tf):
    rows = x.shape[0]
    vec = pl.BlockSpec((1, D_MODEL), lambda i, f: (0, 0))
    return pl.pallas_call(
        _ffn_kernel,
        out_shape=jax.ShapeDtypeStruct((rows, D_MODEL), F32),
        grid=(rows // tm, D_FF // tf),
        in_specs=[pl.BlockSpec((tm, D_MODEL), lambda i, f: (i, 0), pipeline_mode=pl.Buffered(1)),
                  pl.BlockSpec((D_MODEL, tf), lambda i, f: (0, f)),
                  pl.BlockSpec((tf, D_MODEL), lambda i, f: (f, 0)),
                  vec, vec],
        out_specs=pl.BlockSpec((tm, D_MODEL), lambda i, f: (i, 0)),
        scratch_shapes=[pltpu.VMEM((tm, D_MODEL), BF16)],
        compiler_params=pltpu.CompilerParams(dimension_semantics=("arbitrary", "arbitrary"),
                                             vmem_limit_bytes=FFN_VMEM_LIMIT),
        name="ffn",
    )(x, wu, wd, g, b)


def _row_tile(m):
    return min(m, 1024)


def _lane_row(vals, offset):
    return jnp.zeros((1, BA_LANES), F32).at[0, offset:offset + vals.shape[0]].set(vals)


def _layer(x, conv_state, ssm_state, wts, batch, seq):
    (w_main, w_gates, w_ba, w_s, b_s, ln_v_g, ln_v_b, w_conv, a_log, dt_bias, w_onorm, w_pa, w_pb, w_o,
     ln1_g, ln1_b, w_up, w_down, ln2_g, ln2_b) = wts
    m = x.shape[0]
    tm = _row_tile(m)
    xb = x.astype(BF16)
    if isinstance(w_main, tuple):
        p, w_main = _inproj(xb, w_main[0], N_MAIN, BF16, 2 * SEG, tm, 1024, layer=w_main[1])
    else:
        p, w_main = _inproj(xb, w_main, N_MAIN, BF16, 2 * SEG, tm, 1024)
    gates, w_gates = _inproj(xb, w_gates, 2 * SEG, BF16, 0, tm, 1024)
    ba, w_ba = _inproj(xb, w_ba, BA_LANES, F32, 0, tm, BA_LANES)

    row = lambda v: v.reshape(1, -1)
    alog_row = _lane_row(a_log, HEADS)
    dt_row = _lane_row(dt_bias, HEADS)
    if conv_state is None:
        (ya,) = _mixer_a(p, w_s, b_s.T, row(ln_v_g), row(ln_v_b), emit_van=False)
        van = None
        yb, ssm_new = _delta_chunks(p, ba, w_conv, alog_row, dt_row, row(w_onorm), batch, seq)
        tail = p.reshape(batch, seq, N_MAIN)[:, seq - (CONV_W - 1):, SEG_Q * SEG:(SEG_V + 1) * SEG]
        conv_new = tail.astype(F32)
    else:
        w_s0 = w_s[:, 0, 0][:, None, None] * jnp.eye(A_CHUNK, dtype=F32)[None]
        b_s0 = jnp.broadcast_to(b_s[:, 0][None, :], (A_CHUNK, GROUPS))
        ya, van = _mixer_a(p, w_s0, b_s0, row(ln_v_g), row(ln_v_b), emit_van=True)
        yb, ssm_new = _delta_step(p, ba, conv_state, ssm_state, w_conv, alog_row, dt_row, row(w_onorm))
        qkv = p[:, SEG_Q * SEG:(SEG_V + 1) * SEG].astype(F32)
        conv_new = jnp.concatenate([conv_state[:, 1:, :], qkv[:, None, :]], axis=1)

    if w_up.dtype != BF16:
        mg, w_up, w_down = _merge(ya, yb, gates, w_pa, w_pb, tm, 512, side=(w_up, w_down))
    else:
        (mg,) = _merge(ya, yb, gates, w_pa, w_pb, tm, 512)
    x1 = _out_ln(x, mg, w_o, row(ln1_g), row(ln1_b), min(m, 512))
    y = _ffn(x1, w_up, w_down, row(ln2_g), row(ln2_b), tm, 512)
    wts = (w_main, w_gates, w_ba) + tuple(wts[3:16]) + (w_up, w_down) + tuple(wts[18:])
    return (y, van, conv_new, ssm_new), wts


def kernel(x_prompt, x_sample, state_conv, state_ssm, w_in, w_s, b_s, ln_v_g, ln_v_b, w_conv, a_log,
           dt_bias, w_onorm, w_proj_a, w_proj_b, w_o, ln1_g, ln1_b, w_up, w_down, ln2_g, ln2_b):
    depth = w_in.shape[0]
    bp, tp, _ = x_prompt.shape
    bs, ts, _ = x_sample.shape
    assert ts == 1, "the sample group advances one token per sequence"
    yp = x_prompt.reshape(bp * tp, D_MODEL)
    ys = x_sample.reshape(bs, D_MODEL)
    conv_p, ssm_p, vrows_s, conv_s, ssm_s = [], [], [], [], []
    for l in range(depth):
        w_in_t = jnp.swapaxes(w_in, 1, 2)
        w_gates = w_in_t[l, N_MAIN + N_BA:, :]
        w_ba = jnp.pad(w_in_t[l, N_MAIN:N_MAIN + N_BA, :], ((0, BA_LANES - N_BA), (0, 0)))
        wts = ((w_in_t, l), w_gates, w_ba, w_s[l], b_s[l], ln_v_g[l], ln_v_b[l], w_conv[l], a_log[l], dt_bias[l],
               w_onorm[l], w_proj_a[l].astype(BF16), w_proj_b[l].astype(BF16), w_o[l].astype(BF16),
               ln1_g[l], ln1_b[l], w_up[l], w_down[l], ln2_g[l], ln2_b[l])
        (yp, _, cp, sp), wts = _layer(yp, None, None, wts, bp, tp)
        (ys, vs, cs, ss), _ = _layer(ys, state_conv[l], state_ssm[l], wts, bs, 1)
        conv_p.append(cp)
        ssm_p.append(sp)
        vrows_s.append(vs.reshape(bs, 1, SEG))
        conv_s.append(cs)
        ssm_s.append(ss)
    return (yp.reshape(bp, tp, D_MODEL), ys.reshape(bs, 1, D_MODEL), jnp.stack(conv_p), jnp.stack(ssm_p),
            jnp.stack(vrows_s), jnp.stack(conv_s), jnp.stack(ssm_s))
```
